```python
import jax, jax.numpy as jnp
from jax import lax
import numpy as np

D_MODEL = 1024
BATCH = 8
SEQ = 2048
DEPTH = 4

GRID_W = 64
CTX_LEN = 256

A_WIDTH = 512
A_GROUPS = 4
CHUNK = 128
MLA_HEADS = 8
QK_NOPE = 64
QK_ROPE = 32
V_DIM = 64
Q_RANK = 256
KV_RANK = 128
ROPE_BASE = 10000.0
Q_BLOCK = 128
C_WIDTH = 512
C_BLOCKS = 8
C_BLOCK_W = C_WIDTH // C_BLOCKS
CONV_W = 4
CONV_LEFT = 2
LRU_C = 8.0
N_BRANCH = 3
BRANCH_W = 512
OFF_AU = 0
OFF_AV = OFF_AU + A_WIDTH
OFF_CQ = OFF_AV + A_WIDTH
OFF_CKV = OFF_CQ + Q_RANK
OFF_KR = OFF_CKV + KV_RANK
OFF_CX = OFF_KR + QK_ROPE
OFF_CG = OFF_CX + C_WIDTH
OFF_GATE = OFF_CG + C_WIDTH
N_IN = OFF_GATE + N_BRANCH * D_MODEL
N_EXPERTS = 16
N_GROUPS = 4
EXPERTS_PER_GROUP = N_EXPERTS // N_GROUPS
TOP_K = 2
D_EXPERT = 512
ROUTED_SCALE = 2.5
EPS = 1e-6

kernel_name = 'hybrid_gated_mixer_moe_trunk'


def layer_norm(x, g, b):
    x32 = x.astype(jnp.float32)
    mu = jnp.mean(x32, axis=-1, keepdims=True)
    var = jnp.mean(jnp.square(x32 - mu), axis=-1, keepdims=True)
    return ((x32 - mu) * lax.rsqrt(var + EPS)).astype(x.dtype) * g + b


def rms_norm(x, g):
    x32 = x.astype(jnp.float32)
    ms = jnp.mean(jnp.square(x32), axis=-1, keepdims=True)
    return (x32 * lax.rsqrt(ms + EPS)).astype(x.dtype) * g


def axial_rope(rows):
    row = jnp.repeat(jnp.arange(rows, dtype=jnp.float32), GRID_W)
    col = jnp.tile(jnp.arange(GRID_W, dtype=jnp.float32), rows)
    n_freq = QK_ROPE // 4
    inv = ROPE_BASE ** (-jnp.arange(n_freq, dtype=jnp.float32) / n_freq)
    ang = jnp.stack([row[:, None] * inv, col[:, None] * inv], axis=1)
    return jnp.cos(ang), jnp.sin(ang)


def apply_rope(x, cos, sin):
    shp = x.shape
    xr = x.reshape(shp[:-1] + (2, 2, QK_ROPE // 4))
    x1, x2 = xr[..., 0, :], xr[..., 1, :]
    c = cos[None, :, None]
    s = sin[None, :, None]
    out = jnp.stack([x1 * c - x2 * s, x2 * c + x1 * s], axis=-2)
    return out.reshape(shp).astype(x.dtype)


def chunk_gmlp(u_raw, v_raw, p):
    bsz, n, _ = v_raw.shape
    u = jax.nn.gelu(u_raw)
    v = layer_norm(jax.nn.gelu(v_raw), p['a_ln_g'], p['a_ln_b'])
    vc = v.reshape(bsz, n // CHUNK, CHUNK, A_GROUPS, A_WIDTH // A_GROUPS)
    mixed = jnp.einsum('gpq,bcqgd->bcpgd', p['w_s'], vc) + p['b_s'].T[None, None, :, :, None]
    return u * mixed.reshape(bsz, n, A_WIDTH)


def mla_queries(cq, p, rope):
    q = rms_norm(cq, p['q_norm_g']) @ p['w_uq']
    q = q.reshape(q.shape[:-1] + (MLA_HEADS, QK_NOPE + QK_ROPE))
    q_nope, q_rope = q[..., :QK_NOPE], q[..., QK_NOPE:]
    if rope is not None:
        q_rope = apply_rope(q_rope, *rope)
    return jnp.concatenate([q_nope, q_rope], axis=-1)


def mla_keys_values(ckv, kr, p, rope):
    kv = rms_norm(ckv, p['kv_norm_g']) @ p['w_ukv']
    kv = kv.reshape(kv.shape[:-1] + (MLA_HEADS, QK_NOPE + V_DIM))
    k_nope, v = kv[..., :QK_NOPE], kv[..., QK_NOPE:]
    k_rope = kr[..., None, :]
    if rope is not None:
        k_rope = apply_rope(k_rope, *rope)
    k_rope = jnp.broadcast_to(k_rope, k_nope.shape[:-1] + (QK_ROPE,))
    return jnp.concatenate([k_nope, k_rope], axis=-1), v


def block_attention(q, k, v):
    bsz, n, h, dq = q.shape
    nb = n // Q_BLOCK
    scale = dq ** -0.5
    qb = q.reshape(bsz, nb, Q_BLOCK, h, dq).transpose(1, 0, 2, 3, 4)

    def one(qblk):
        s = jnp.einsum('bqhd,bkhd->bhqk', qblk, k).astype(jnp.float32) * scale
        pr = jax.nn.softmax(s, axis=-1).astype(v.dtype)
        return jnp.einsum('bhqk,bkhd->bqhd', pr, v)

    out = lax.map(one, qb)
    return out.transpose(1, 0, 2, 3, 4).reshape(bsz, n, h * v.shape[-1])


def depthwise_conv(x, w, b):
    n = x.shape[1]
    xp = jnp.pad(x, ((0, 0), (CONV_LEFT, CONV_W - 1 - CONV_LEFT), (0, 0)))
    out = b
    for k in range(CONV_W):
        out = out + xp[:, k:k + n] * w[k]
    return out


def block_diag(x, w, b):
    xb = x.reshape(x.shape[:-1] + (C_BLOCKS, C_BLOCK_W))
    return jnp.einsum('bnhi,hij->bnhj', xb, w).reshape(x.shape) + b


def rglru_coeffs(x, w_r, b_r, w_i, b_i, lam):
    r = jax.nn.sigmoid(block_diag(x, w_r, b_r))
    i = jax.nn.sigmoid(block_diag(x, w_i, b_i))
    log_a = LRU_C * r * jax.nn.log_sigmoid(lam)
    a = jnp.exp(log_a)
    return a, jnp.sqrt(-jnp.expm1(2.0 * log_a)) * (i * x)


def linear_scan(a, bx, h0, reverse):
    first = -1 if reverse else 0
    bx = bx.at[:, first].add(a[:, first] * h0)

    def comb(l, r):
        return (l[0] * r[0], r[0] * l[1] + r[1])

    _, h = lax.associative_scan(comb, (a, bx), axis=1, reverse=reverse)
    return h


def merge(y_a, y_b, y_c, gate_raw, p):
    g = jax.nn.sigmoid(gate_raw.astype(jnp.float32)).astype(y_a.dtype)
    g = g.reshape(gate_raw.shape[:-1] + (N_BRANCH, D_MODEL))
    ys = jnp.stack([y_a, y_b, y_c], axis=-2)
    br = jnp.einsum('bnkc,kcd->bnkd', ys, p['w_br'])
    return jnp.sum(g * br, axis=-2) @ p['w_o']


def token_mixer(h_lat, h_ctx, p, rope, ctx_out):
    pl = h_lat @ p['w_in']
    if ctx_out:
        pc, base = h_ctx @ p['w_in'], 0
    else:
        pc, base = h_ctx @ p['w_in'][:, OFF_CKV:OFF_CG], OFF_CKV

    def lat(off, width):
        return pl[..., off:off + width]

    def ctxc(off, width):
        return pc[..., off - base:off - base + width]

    y_a_lat = chunk_gmlp(lat(OFF_AU, A_WIDTH), lat(OFF_AV, A_WIDTH), p)
    k_ctx, v_ctx = mla_keys_values(ctxc(OFF_CKV, KV_RANK), ctxc(OFF_KR, QK_ROPE), p, None)
    k_lat, v_lat = mla_keys_values(lat(OFF_CKV, KV_RANK), lat(OFF_KR, QK_ROPE), p, rope)
    q_lat = mla_queries(lat(OFF_CQ, Q_RANK), p, rope)
    y_b_lat = block_attention(q_lat, jnp.concatenate([k_ctx, k_lat], axis=1),
                              jnp.concatenate([v_ctx, v_lat], axis=1))
    xc_ctx = depthwise_conv(ctxc(OFF_CX, C_WIDTH).astype(jnp.float32), p['conv_w'], p['conv_b'])
    xc_lat = depthwise_conv(lat(OFF_CX, C_WIDTH).astype(jnp.float32), p['conv_w'], p['conv_b'])
    rec_lat = 0.0
    rec_ctx = 0.0
    for d, reverse in enumerate((False, True)):
        coeff = (p['w_r'][d], p['b_r'][d], p['w_i'][d], p['b_i'][d], p['lru_lambda'][d])
        a, bx = rglru_coeffs(xc_ctx, *coeff)
        hc = linear_scan(a, bx, jnp.zeros_like(xc_ctx[:, 0]), reverse)
        h_last = hc[:, 0] if reverse else hc[:, -1]
        a, bx = rglru_coeffs(xc_lat, *coeff)
        rec_lat = rec_lat + linear_scan(a, bx, h_last, reverse)
        if ctx_out:
            rec_ctx = rec_ctx + hc
    y_c_lat = jax.nn.gelu(lat(OFF_CG, C_WIDTH)) * rec_lat.astype(h_lat.dtype)
    y_lat = merge(y_a_lat, y_b_lat, y_c_lat, lat(OFF_GATE, N_BRANCH * D_MODEL), p)
    if not ctx_out:
        return y_lat, None
    y_a_ctx = chunk_gmlp(ctxc(OFF_AU, A_WIDTH), ctxc(OFF_AV, A_WIDTH), p)
    y_b_ctx = block_attention(mla_queries(ctxc(OFF_CQ, Q_RANK), p, None), k_ctx, v_ctx)
    y_c_ctx = jax.nn.gelu(ctxc(OFF_CG, C_WIDTH)) * rec_ctx.astype(h_ctx.dtype)
    y_ctx = merge(y_a_ctx, y_b_ctx, y_c_ctx, ctxc(OFF_GATE, N_BRANCH * D_MODEL), p)
    return y_lat, y_ctx


def moe(h, w_router, router_bias, w_gate, w_up, w_down):
    n_tok = h.shape[0]
    scores = jax.nn.sigmoid((h @ w_router).astype(jnp.float32))
    sel = (scores + router_bias.astype(jnp.float32)).reshape(n_tok, N_GROUPS, EXPERTS_PER_GROUP)
    group_score = jnp.sum(lax.top_k(sel, 2)[0], axis=-1)
    best = jnp.argmax(group_score, axis=-1)
    in_group = jnp.take_along_axis(sel, best[:, None, None], axis=1)[:, 0]
    _, local = lax.top_k(in_group, TOP_K)
    expert_idx = best[:, None] * EXPERTS_PER_GROUP + local
    w = jnp.take_along_axis(scores, expert_idx, axis=-1)
    w = ROUTED_SCALE * w / jnp.sum(w, axis=-1, keepdims=True)
    combine = jnp.sum(jax.nn.one_hot(expert_idx, N_EXPERTS, dtype=jnp.float32) * w[..., None], axis=1)
    combine = combine.astype(h.dtype)

    def expert(acc, ew):
        wg, wu, wd, cw = ew
        y = (jax.nn.silu(h @ wg) * (h @ wu)) @ wd
        return acc + cw[:, None] * y, None

    out, _ = lax.scan(expert, jnp.zeros_like(h), (w_gate, w_up, w_down, combine.T))
    return out


def setup_inputs(seed: int = 0) -> dict:
    key = jax.random.key(seed)
    ks = iter(jax.random.split(key, 40))
    f32 = jnp.float32
    beta = (8.0 * DEPTH) ** -0.25

    def nrm(shape, scale):
        return jax.random.normal(next(ks), shape, f32) * scale

    def gain(shape):
        return 1.0 + nrm(shape, 0.02)

    a0 = jax.random.uniform(next(ks), (DEPTH, 2, C_WIDTH), f32, 0.9, 0.999)
    s0 = a0 ** (1.0 / LRU_C)
    lru_lambda = jnp.log(s0) - jnp.log1p(-s0)
    return {
        'x': nrm((BATCH, SEQ, D_MODEL), 1.0),
        'c': nrm((BATCH, D_MODEL), 1.0),
        'ctx': nrm((BATCH, CTX_LEN, D_MODEL), 1.0),
        'c_ctx': nrm((D_MODEL,), 1.0),
        'w_ada': nrm((DEPTH, D_MODEL, 6 * D_MODEL), 0.5 * D_MODEL ** -0.5),
        'b_ada': nrm((DEPTH, 6 * D_MODEL), 0.02),
        'w_in': nrm((DEPTH, D_MODEL, N_IN), D_MODEL ** -0.5),
        'a_ln_g': gain((DEPTH, A_WIDTH)),
        'a_ln_b': nrm((DEPTH, A_WIDTH), 0.02),
        'w_s': nrm((DEPTH, A_GROUPS, CHUNK, CHUNK), CHUNK ** -0.5),
        'b_s': gain((DEPTH, A_GROUPS, CHUNK)),
        'q_norm_g': gain((DEPTH, Q_RANK)),
        'w_uq': nrm((DEPTH, Q_RANK, MLA_HEADS * (QK_NOPE + QK_ROPE)), Q_RANK ** -0.5),
        'kv_norm_g': gain((DEPTH, KV_RANK)),
        'w_ukv': nrm((DEPTH, KV_RANK, MLA_HEADS * (QK_NOPE + V_DIM)), KV_RANK ** -0.5),
        'conv_w': nrm((DEPTH, CONV_W, C_WIDTH), CONV_W ** -0.5),
        'conv_b': nrm((DEPTH, C_WIDTH), 0.02),
        'w_r': nrm((DEPTH, 2, C_BLOCKS, C_BLOCK_W, C_BLOCK_W), C_BLOCK_W ** -0.5),
        'b_r': nrm((DEPTH, 2, C_WIDTH), 0.02),
        'w_i': nrm((DEPTH, 2, C_BLOCKS, C_BLOCK_W, C_BLOCK_W), C_BLOCK_W ** -0.5),
        'b_i': nrm((DEPTH, 2, C_WIDTH), 0.02),
        'lru_lambda': lru_lambda,
        'w_br': nrm((DEPTH, N_BRANCH, BRANCH_W, D_MODEL), BRANCH_W ** -0.5),
        'w_o': nrm((DEPTH, D_MODEL, D_MODEL), beta * D_MODEL ** -0.5),
        'ln1_g': gain((DEPTH, D_MODEL)),
        'ln1_b': nrm((DEPTH, D_MODEL), 0.02),
        'w_router': nrm((D_MODEL, N_EXPERTS), D_MODEL ** -0.5),
        'router_bias': nrm((N_EXPERTS,), 0.01),
        'w_gate': nrm((DEPTH, N_EXPERTS, D_MODEL, D_EXPERT), D_MODEL ** -0.5),
        'w_up': nrm((DEPTH, N_EXPERTS, D_MODEL, D_EXPERT), D_MODEL ** -0.5),
        'w_down': nrm((DEPTH, N_EXPERTS, D_EXPERT, D_MODEL), beta * D_EXPERT ** -0.5),
        'ln2_g': gain((DEPTH, D_MODEL)),
        'ln2_b': nrm((DEPTH, D_MODEL), 0.02),
    }


def reference(x, c, ctx, c_ctx, w_ada, b_ada, w_in, a_ln_g, a_ln_b, w_s, b_s, q_norm_g, w_uq,
              kv_norm_g, w_ukv, conv_w, conv_b, w_r, b_r, w_i, b_i, lru_lambda, w_br, w_o,
              ln1_g, ln1_b, w_router, router_bias, w_gate, w_up, w_down, ln2_g, ln2_b):
    alpha = (2.0 * DEPTH) ** 0.25
    bsz, n_lat, d = x.shape
    rows = n_lat // GRID_W
    rope = axial_rope(rows)
    cond_lat = jax.nn.silu(c)
    cond_ctx = jax.nn.silu(c_ctx)
    x_lat, x_ctx = x, ctx
    for l in range(DEPTH):
        ctx_out = l < DEPTH - 1
        p = {
            'w_in': w_in[l], 'a_ln_g': a_ln_g[l], 'a_ln_b': a_ln_b[l], 'w_s': w_s[l], 'b_s': b_s[l],
            'q_norm_g': q_norm_g[l], 'w_uq': w_uq[l], 'kv_norm_g': kv_norm_g[l], 'w_ukv': w_ukv[l],
            'conv_w': conv_w[l], 'conv_b': conv_b[l], 'w_r': w_r[l], 'b_r': b_r[l], 'w_i': w_i[l],
            'b_i': b_i[l], 'lru_lambda': lru_lambda[l], 'w_br': w_br[l], 'w_o': w_o[l],
        }
        mod_lat = (cond_lat @ w_ada[l] + b_ada[l])[:, None, :]
        mod_ctx = (cond_ctx @ w_ada[l] + b_ada[l])[None, None, :]
        sh1, sc1, g1, sh2, sc2, g2 = jnp.split(mod_lat, 6, axis=-1)
        csh1, csc1, cg1, csh2, csc2, cg2 = jnp.split(mod_ctx, 6, axis=-1)
        h_lat = x_lat * (1.0 + sc1) + sh1
        h_ctx = x_ctx * (1.0 + csc1) + csh1
        y_lat, y_ctx = token_mixer(h_lat, h_ctx, p, rope, ctx_out)
        x_lat = layer_norm(alpha * x_lat + g1 * y_lat, ln1_g[l], ln1_b[l])
        f_in_lat = (x_lat * (1.0 + sc2) + sh2).reshape(-1, d)
        if ctx_out:
            x_ctx = layer_norm(alpha * x_ctx + cg1 * y_ctx, ln1_g[l], ln1_b[l])
            f_in_ctx = (x_ctx * (1.0 + csc2) + csh2).reshape(-1, d)
            f_out = moe(jnp.concatenate([f_in_lat, f_in_ctx], axis=0), w_router, router_bias,
                        w_gate[l], w_up[l], w_down[l])
            f_lat = f_out[:f_in_lat.shape[0]].reshape(x_lat.shape)
            f_ctx = f_out[f_in_lat.shape[0]:].reshape(x_ctx.shape)
            x_ctx = layer_norm(alpha * x_ctx + cg2 * f_ctx, ln2_g[l], ln2_b[l])
        else:
            f_lat = moe(f_in_lat, w_router, router_bias, w_gate[l], w_up[l], w_down[l]).reshape(x_lat.shape)
        x_lat = layer_norm(alpha * x_lat + g2 * f_lat, ln2_g[l], ln2_b[l])
    return x_lat
```

```python
import functools

import numpy as np
import jax
import jax.numpy as jnp
from jax import lax
from jax.experimental import pallas as pl
from jax.experimental.pallas import tpu as pltpu

D = 1024
GRID_W = 64
A_WIDTH = 512
A_GROUPS = 4
CHUNK = 128
HEADS = 8
QK_NOPE = 64
QK_ROPE = 32
V_DIM = 64
Q_RANK = 256
KV_RANK = 128
ROPE_BASE = 10000.0
C_WIDTH = 512
C_BLOCKS = 8
C_BLOCK_W = C_WIDTH // C_BLOCKS
CONV_W = 4
CONV_LEFT = 2
LRU_C = 8.0
N_BRANCH = 3
BRANCH_W = 512
OFF_AU = 0
OFF_AV = OFF_AU + A_WIDTH
OFF_CQ = OFF_AV + A_WIDTH
OFF_CKV = OFF_CQ + Q_RANK
OFF_KR = OFF_CKV + KV_RANK
OFF_CX = OFF_KR + QK_ROPE
OFF_CG = OFF_CX + C_WIDTH
OFF_GATE = OFF_CG + C_WIDTH
N_EXPERTS = 16
N_GROUPS = 4
EXPERTS_PER_GROUP = N_EXPERTS // N_GROUPS
D_EXPERT = 512
ROUTED_SCALE = 2.5
EPS = 1e-6

LANES = 128
SUBLANES = 8
TILE = 256
HEAD_PAD = 128
ROPE_LANE0 = QK_NOPE
MOE_TILE = 1024
VMEM_LIMIT = 56 * 1024 * 1024

BF16 = jnp.bfloat16
F32 = jnp.float32


def _cparams(sem):
    return pltpu.CompilerParams(dimension_semantics=sem, vmem_limit_bytes=VMEM_LIMIT)


def _dot(a, b):
    return jnp.dot(a, b, preferred_element_type=F32)


def _layer_norm(v, g, b):
    mu = jnp.mean(v, axis=-1, keepdims=True)
    c = v - mu
    var = jnp.mean(c * c, axis=-1, keepdims=True)
    return (c * lax.rsqrt(var + EPS)) * g + b


def _rms_norm(v, g):
    ms = jnp.mean(v * v, axis=-1, keepdims=True)
    return (v * lax.rsqrt(ms + EPS)) * g


def _mod_kernel(c_ref, w_ref, b_ref, o_ref):
    c = c_ref[...]
    cond = (c * jax.nn.sigmoid(c)).astype(BF16)
    o_ref[0] = _dot(cond, w_ref[0].astype(BF16)) + b_ref[0]


def _mod_call(cond_rows, w_ada, b_ada):
    depth = w_ada.shape[0]
    rows = cond_rows.shape[0]
    ncol = 6 * D
    bn = 1536
    return pl.pallas_call(
        _mod_kernel,
        grid=(depth, ncol // bn),
        in_specs=[
            pl.BlockSpec((rows, D), lambda l, j: (0, 0)),
            pl.BlockSpec((1, D, bn), lambda l, j: (l, 0, j)),
            pl.BlockSpec((1, 1, bn), lambda l, j: (l, 0, j)),
        ],
        out_specs=pl.BlockSpec((1, rows, bn), lambda l, j: (l, 0, j)),
        out_shape=jax.ShapeDtypeStruct((depth, rows, ncol), F32),
        compiler_params=_cparams(("parallel", "parallel")),
        name="mod",
    )(cond_rows, w_ada, b_ada.reshape(depth, 1, ncol))


def _proj_kernel(x_ref, mod_ref, cos_ref, sin_ref, wa_ref, wb_ref, wc_ref,
                 alng_ref, alnb_ref, ws_ref, bs_ref, qg_ref, wuq_ref, kvg_ref,
                 wuk_ref, wuv_ref,
                 h_ref, ya_ref, q_ref, k_ref, v_ref, xc_ref, cg_ref):
    x = x_ref[...]
    m = mod_ref[0]
    sh1 = m[:, 0:D]
    sc1 = m[:, D:2 * D]
    h = (x * (1.0 + sc1) + sh1).astype(BF16)
    h_ref[...] = h

    uv = _dot(h, wa_ref[0])
    u = jax.nn.gelu(uv[:, :A_WIDTH])
    v = _layer_norm(jax.nn.gelu(uv[:, A_WIDTH:]), alng_ref[0], alnb_ref[0]).astype(BF16)
    gw = A_WIDTH // A_GROUPS
    for c in range(TILE // CHUNK):
        rows = slice(c * CHUNK, (c + 1) * CHUNK)
        for g in range(A_GROUPS):
            cols = slice(g * gw, (g + 1) * gw)
            mixed = _dot(ws_ref[0, g], v[rows, cols]) + bs_ref[0, :, cols]
            ya_ref[rows, cols] = (u[rows, cols] * mixed).astype(BF16)

    pb = _dot(h, wb_ref[0])
    cos_t = cos_ref[...]
    sin_t = sin_ref[...]
    cqn = _rms_norm(pb[:, :Q_RANK], qg_ref[0]).astype(BF16)
    qq = _dot(cqn, wuq_ref[0])
    hw = HEADS * HEAD_PAD
    for hd in range(HEADS):
        a = slice(hd * HEAD_PAD, (hd + 1) * HEAD_PAD)
        b = slice(hw + hd * HEAD_PAD, hw + (hd + 1) * HEAD_PAD)
        q_ref[:, a] = (qq[:, a] * cos_t + qq[:, b] * sin_t).astype(BF16)
    ckvn = _rms_norm(pb[:, Q_RANK:Q_RANK + KV_RANK], kvg_ref[0]).astype(BF16)
    o_kr = Q_RANK + KV_RANK
    kr = pb[:, o_kr:o_kr + HEAD_PAD] * cos_t + pb[:, o_kr + HEAD_PAD:o_kr + 2 * HEAD_PAD] * sin_t
    kk = _dot(ckvn, wuk_ref[0])
    for hd in range(HEADS):
        a = slice(hd * HEAD_PAD, (hd + 1) * HEAD_PAD)
        k_ref[:, a] = (kk[:, a] + kr).astype(BF16)
    v_ref[...] = _dot(ckvn, wuv_ref[0]).astype(BF16)

    pc = _dot(h, wc_ref[0])
    xc_ref[...] = pc[:, :C_WIDTH]
    cg_ref[...] = jax.nn.gelu(pc[:, C_WIDTH:])


def _proj_call(l, x, mod_tiles, cos_t, sin_t, w, nct):
    n = x.shape[0]
    nt = n // TILE
    tok = lambda width: pl.BlockSpec((TILE, width), lambda i: (i, 0))
    lay3 = lambda a: pl.BlockSpec((1,) + a.shape[1:], lambda i: (l, 0, 0))
    lay4 = lambda a: pl.BlockSpec((1,) + a.shape[1:], lambda i: (l, 0, 0, 0))
    pos = pl.BlockSpec((TILE, HEAD_PAD), lambda i: (i % nct, 0))
    outs = [
        (D, BF16), (A_WIDTH, BF16), (HEADS * HEAD_PAD, BF16), (HEADS * HEAD_PAD, BF16),
        (HEADS * V_DIM, BF16), (C_WIDTH, F32), (C_WIDTH, F32),
    ]
    return pl.pallas_call(
        _proj_kernel,
        grid=(nt,),
        in_specs=[
            tok(D),
            pl.BlockSpec((1, 1, 6 * D), lambda i: (i, 0, 0)),
            pos, pos,
            lay3(w["wa"]), lay3(w["wb"]), lay3(w["wc"]),
            lay3(w["a_ln_g"]), lay3(w["a_ln_b"]), lay4(w["w_s"]), lay3(w["bs_full"]),
            lay3(w["q_norm_g"]), lay3(w["wuq"]), lay3(w["kv_norm_g"]),
            lay3(w["wuk"]), lay3(w["wuv"]),
        ],
        out_specs=[tok(wd) for wd, _ in outs],
        out_shape=[jax.ShapeDtypeStruct((n, wd), dt) for wd, dt in outs],
        compiler_params=_cparams(("parallel",)),
        name="proj",
    )(x, mod_tiles, cos_t, sin_t, w["wa"], w["wb"], w["wc"], w["a_ln_g"], w["a_ln_b"],
      w["w_s"], w["bs_full"], w["q_norm_g"], w["wuq"], w["kv_norm_g"], w["wuk"], w["wuv"])


def _attn_kernel(q_ref, k_ref, v_ref, o_ref):
    j = pl.program_id(2)
    scale = float(QK_NOPE + QK_ROPE) ** -0.5
    lane = lax.broadcasted_iota(jnp.int32, (TILE, 2 * V_DIM), 1)

    def attend(nk):
        vv = v_ref[:nk, :]
        outs = []
        for hd in range(2):
            cols = slice(hd * HEAD_PAD, (hd + 1) * HEAD_PAD)
            s = lax.dot_general(q_ref[:, cols], k_ref[:nk, cols],
                                (((1,), (1,)), ((), ())),
                                preferred_element_type=F32) * scale
            mx = jnp.max(s, axis=-1, keepdims=True)
            p = jnp.exp(s - mx)
            den = jnp.sum(p, axis=-1, keepdims=True)
            outs.append(_dot(p.astype(BF16), vv) / den)
        o_ref[...] = jnp.where(lane < V_DIM, outs[0], outs[1]).astype(BF16)

    @pl.when(j == 0)
    def _():
        attend(TILE)

    @pl.when(j > 0)
    def _():
        attend(k_ref.shape[0])


def _attn_call(q, k, v, bsz, nct):
    n = q.shape[0]
    t = nct * TILE
    npair = HEADS // 2
    return pl.pallas_call(
        _attn_kernel,
        grid=(bsz, npair, nct),
        in_specs=[
            pl.BlockSpec((TILE, 2 * HEAD_PAD), lambda b, p, j: (b * nct + j, p)),
            pl.BlockSpec((t, 2 * HEAD_PAD), lambda b, p, j: (b, p)),
            pl.BlockSpec((t, 2 * V_DIM), lambda b, p, j: (b, p)),
        ],
        out_specs=pl.BlockSpec((TILE, 2 * V_DIM), lambda b, p, j: (b * nct + j, p)),
        out_shape=jax.ShapeDtypeStruct((n, HEADS * V_DIM), BF16),
        compiler_params=_cparams(("parallel", "parallel", "arbitrary")),
        name="attn",
    )(q, k, v)


def _scan_chunk(d, s, nct):
    return jnp.where(d == 0, s, jnp.where(s == 0, 0, nct - s))


def _scan_kernel(xc_ref, prev_ref, next_ref, cw_ref, cb_ref, wri_ref, bri_ref, lam_ref,
                 o_ref, xpad, a_s, bx_s, hs_s, carry, *, nct):
    d = pl.program_id(0)
    s = pl.program_id(1)
    chunk = _scan_chunk(d, s, nct)
    bsz = xc_ref.shape[0]
    halo = SUBLANES
    has_prev = jnp.logical_and(chunk != 0, chunk != 1)
    has_next = jnp.logical_and(chunk != 0, chunk != nct - 1)
    xpad[:, halo:halo + TILE, :] = xc_ref[...]
    xpad[:, 0:halo, :] = jnp.where(has_prev, prev_ref[...], 0.0)
    xpad[:, halo + TILE:halo + TILE + halo, :] = jnp.where(has_next, next_ref[...], 0.0)
    xconv = cb_ref[...]
    for kk in range(CONV_W):
        off = halo - CONV_LEFT + kk
        xconv = xconv + xpad[:, off:off + TILE, :] * cw_ref[0, kk:kk + 1, :]
    x2 = xconv.reshape(bsz * TILE, C_WIDTH)
    ri = _dot(x2.astype(BF16), wri_ref[0]) + bri_ref[0]
    r = jax.nn.sigmoid(ri[:, :C_WIDTH])
    ig = jax.nn.sigmoid(ri[:, C_WIDTH:])
    lam = lam_ref[0]
    log_sig = jnp.minimum(lam, 0.0) - jnp.log1p(jnp.exp(-jnp.abs(lam)))
    log_a = LRU_C * r * log_sig
    a = jnp.exp(log_a)
    bx = jnp.sqrt(-jnp.tanh(log_a) * (a * a + 1.0)) * (ig * x2)
    nlb = C_WIDTH // LANES
    for cb in range(nlb):
        cols = slice(cb * LANES, (cb + 1) * LANES)
        a_s[cb] = a[:, cols]
        bx_s[cb] = bx[:, cols]

    @pl.when(s == 0)
    def _():
        carry[...] = jnp.zeros_like(carry)

    def step(t, hprev):
        tt = jnp.where(d == 0, t, TILE - 1 - t)
        idx = pl.ds(tt, bsz, stride=TILE)
        hnew = []
        for cb in range(nlb):
            hc = a_s[cb, idx, :] * hprev[cb] + bx_s[cb, idx, :]
            hs_s[cb, idx, :] = hc
            hnew.append(hc)
        return tuple(hnew)

    h0 = tuple(carry[cb] for cb in range(nlb))
    hfin = lax.fori_loop(0, TILE, step, h0, unroll=8)
    for cb in range(nlb):
        carry[cb] = hfin[cb]
        o_ref[0, :, :, cb * LANES:(cb + 1) * LANES] = hs_s[cb].reshape(bsz, TILE, LANES)


def _scan_call(l, xc3, w, nct):
    bsz, t, _ = xc3.shape
    hb = TILE // SUBLANES
    nhb = t // SUBLANES
    cur = lambda d, s: (0, _scan_chunk(d, s, nct), 0)
    prv = lambda d, s: (0, jnp.maximum(_scan_chunk(d, s, nct) * hb - 1, 0), 0)
    nxt = lambda d, s: (0, jnp.minimum((_scan_chunk(d, s, nct) + 1) * hb, nhb - 1), 0)
    return pl.pallas_call(
        functools.partial(_scan_kernel, nct=nct),
        grid=(2, nct),
        in_specs=[
            pl.BlockSpec((bsz, TILE, C_WIDTH), cur),
            pl.BlockSpec((bsz, SUBLANES, C_WIDTH), prv),
            pl.BlockSpec((bsz, SUBLANES, C_WIDTH), nxt),
            pl.BlockSpec((1, CONV_W, C_WIDTH), lambda d, s: (l, 0, 0)),
            pl.BlockSpec((1, 1, C_WIDTH), lambda d, s: (l, 0, 0)),
            pl.BlockSpec((1, C_WIDTH, 2 * C_WIDTH), lambda d, s: (2 * l + d, 0, 0)),
            pl.BlockSpec((1, 1, 2 * C_WIDTH), lambda d, s: (2 * l + d, 0, 0)),
            pl.BlockSpec((1, 1, C_WIDTH), lambda d, s: (2 * l + d, 0, 0)),
        ],
        out_specs=pl.BlockSpec((1, bsz, TILE, C_WIDTH),
                               lambda d, s: (d, 0, _scan_chunk(d, s, nct), 0)),
        out_shape=jax.ShapeDtypeStruct((2, bsz, t, C_WIDTH), F32),
        scratch_shapes=[
            pltpu.VMEM((bsz, TILE + 2 * SUBLANES, C_WIDTH), F32),
            pltpu.VMEM((C_WIDTH // LANES, bsz * TILE, LANES), F32),
            pltpu.VMEM((C_WIDTH // LANES, bsz * TILE, LANES), F32),
            pltpu.VMEM((C_WIDTH // LANES, bsz * TILE, LANES), F32),
            pltpu.VMEM((C_WIDTH // LANES, bsz, LANES), F32),
        ],
        compiler_params=_cparams(("arbitrary", "arbitrary")),
        name="scan",
    )(xc3, xc3, xc3, w["conv_w"], w["conv_b"], w["wri"], w["bri"], w["lam"])


def _top2_sum(v):
    hi1, lo1 = jnp.maximum(v[0], v[1]), jnp.minimum(v[0], v[1])
    hi2, lo2 = jnp.maximum(v[2], v[3]), jnp.minimum(v[2], v[3])
    top = jnp.maximum(hi1, hi2)
    second = jnp.maximum(jnp.minimum(hi1, hi2), jnp.maximum(lo1, lo2))
    return top + second


def _route(scores, sel):
    gs = [_top2_sum(sel[g * EXPERTS_PER_GROUP:(g + 1) * EXPERTS_PER_GROUP]) for g in range(N_GROUPS)]
    best = jnp.zeros_like(gs[0], dtype=jnp.int32)
    best_v = gs[0]
    for g in range(1, N_GROUPS):
        upd = gs[g] > best_v
        best = jnp.where(upd, g, best)
        best_v = jnp.where(upd, gs[g], best_v)

    def pick(rows, k):
        out = rows[k]
        for g in range(1, N_GROUPS):
            out = jnp.where(best == g, rows[g * EXPERTS_PER_GROUP + k], out)
        return out

    in_sel = [pick(sel, k) for k in range(EXPERTS_PER_GROUP)]
    in_sc = [pick(scores, k) for k in range(EXPERTS_PER_GROUP)]
    neg = jnp.full_like(in_sel[0], -jnp.inf)

    def argmax4(vals):
        idx = jnp.zeros_like(best)
        val = vals[0]
        for k in range(1, EXPERTS_PER_GROUP):
            upd = vals[k] > val
            idx = jnp.where(upd, k, idx)
            val = jnp.where(upd, vals[k], val)
        return idx

    i1 = argmax4(in_sel)
    i2 = argmax4([jnp.where(i1 == k, neg, in_sel[k]) for k in range(EXPERTS_PER_GROUP)])

    def take(vals, idx):
        out = vals[0]
        for k in range(1, EXPERTS_PER_GROUP):
            out = jnp.where(idx == k, vals[k], out)
        return out

    w1 = take(in_sc, i1)
    w2 = take(in_sc, i2)
    den = w1 + w2
    return (best * EXPERTS_PER_GROUP + i1, best * EXPERTS_PER_GROUP + i2,
            ROUTED_SCALE * w1 / den, ROUTED_SCALE * w2 / den)


def _merge_kernel(x_ref, h_ref, ya_ref, yb_ref, cg_ref, rf_ref, rb_ref, mod_ref,
                  wg_ref, wbr_ref, wo_ref, lng_ref, lnb_ref, wrt_ref, rbias_ref,
                  x1_ref, f_ref, comb_ref, *, alpha):
    m = mod_ref[0]
    g1 = m[:, 2 * D:3 * D]
    sh2 = m[:, 3 * D:4 * D]
    sc2 = m[:, 4 * D:5 * D]
    h = h_ref[...]
    yc = (cg_ref[...] * (rf_ref[0] + rb_ref[0])).astype(BF16)
    ys = (ya_ref[...], yb_ref[...], yc)
    mix = None
    for kb in range(N_BRANCH):
        gate = jax.nn.sigmoid(_dot(h, wg_ref[0, :, kb * D:(kb + 1) * D]))
        term = gate * _dot(ys[kb], wbr_ref[0, kb])
        mix = term if mix is None else mix + term
    y = _dot(mix.astype(BF16), wo_ref[0])
    x1 = _layer_norm(alpha * x_ref[...] + g1 * y, lng_ref[0], lnb_ref[0])
    x1_ref[...] = x1
    f = x1 * (1.0 + sc2) + sh2
    f_ref[...] = f.astype(BF16)

    logits = lax.dot_general(wrt_ref[...], f, (((1,), (1,)), ((), ())),
                             precision=lax.Precision.HIGHEST, preferred_element_type=F32)
    sc = jax.nn.sigmoid(logits)
    sl = sc + rbias_ref[...]
    scores = [sc[e:e + 1, :] for e in range(N_EXPERTS)]
    sel = [sl[e:e + 1, :] for e in range(N_EXPERTS)]
    e1, e2, w1, w2 = _route(scores, sel)
    zero = jnp.zeros_like(w1)
    rows = [jnp.where(e1 == e, w1, zero) + jnp.where(e2 == e, w2, zero) for e in range(N_EXPERTS)]
    comb_ref[...] = jnp.concatenate(rows, axis=0)


def _merge_call(l, x, h, ya, yb, cg, rec, mod_tiles, w, wrt, rbias, alpha):
    n = x.shape[0]
    nt = n // TILE
    tok = lambda width: pl.BlockSpec((TILE, width), lambda i: (i, 0))
    lay3 = lambda a: pl.BlockSpec((1,) + a.shape[1:], lambda i: (l, 0, 0))
    lay4 = lambda a: pl.BlockSpec((1,) + a.shape[1:], lambda i: (l, 0, 0, 0))
    return pl.pallas_call(
        functools.partial(_merge_kernel, alpha=alpha),
        grid=(nt,),
        in_specs=[
            tok(D), tok(D), tok(A_WIDTH), tok(HEADS * V_DIM), tok(C_WIDTH),
            pl.BlockSpec((1, TILE, C_WIDTH), lambda i: (0, i, 0)),
            pl.BlockSpec((1, TILE, C_WIDTH), lambda i: (1, i, 0)),
            pl.BlockSpec((1, 1, 6 * D), lambda i: (i, 0, 0)),
            lay3(w["wg"]), lay4(w["w_br"]), lay3(w["w_o"]), lay3(w["ln1_g"]), lay3(w["ln1_b"]),
            pl.BlockSpec(wrt.shape, lambda i: (0, 0)),
            pl.BlockSpec(rbias.shape, lambda i: (0, 0)),
        ],
        out_specs=[tok(D), tok(D), pl.BlockSpec((N_EXPERTS, TILE), lambda i: (0, i))],
        out_shape=[jax.ShapeDtypeStruct((n, D), F32), jax.ShapeDtypeStruct((n, D), BF16),
                   jax.ShapeDtypeStruct((N_EXPERTS, n), F32)],
        compiler_params=_cparams(("parallel",)),
        name="merge",
    )(x, h, ya, yb, cg, rec, rec, mod_tiles, w["wg"], w["w_br"], w["w_o"], w["ln1_g"], w["ln1_b"],
      wrt, rbias)


def _moe_kernel(f_ref, comb_ref, x1_ref, mod_ref, wgt_ref, wup_ref, wdn_ref, lng_ref, lnb_ref,
                o_ref, acc, *, alpha):
    e = pl.program_id(1)

    @pl.when(e == 0)
    def _():
        acc[...] = jnp.zeros_like(acc)

    f = f_ref[...]
    t = (jax.nn.silu(_dot(f, wgt_ref[0, 0])) * _dot(f, wup_ref[0, 0])).astype(BF16)
    y = _dot(t, wdn_ref[0, 0])
    comb = comb_ref[...]
    lane = lax.broadcasted_iota(jnp.int32, comb.shape, 1)
    cw = jnp.sum(jnp.where(lane == e, comb, 0.0), axis=1, keepdims=True)
    acc[...] += cw * y

    @pl.when(e == pl.num_programs(1) - 1)
    def _():
        nsub = MOE_TILE // TILE
        for sub in range(nsub):
            rows = slice(sub * TILE, (sub + 1) * TILE)
            g2 = mod_ref[sub, :, 5 * D:6 * D]
            o_ref[rows, :] = _layer_norm(alpha * x1_ref[rows, :] + g2 * acc[rows, :],
                                         lng_ref[0], lnb_ref[0])


def _moe_call(l, f, comb, x1, mod_tiles, w, alpha):
    n = f.shape[0]
    nt = n // MOE_TILE
    nsub = MOE_TILE // TILE
    tok = lambda width: pl.BlockSpec((MOE_TILE, width), lambda i, e: (i, 0))
    lay3 = lambda a: pl.BlockSpec((1,) + a.shape[1:], lambda i, e: (l, 0, 0))
    return pl.pallas_call(
        functools.partial(_moe_kernel, alpha=alpha),
        grid=(nt, N_EXPERTS),
        in_specs=[
            tok(D), tok(N_EXPERTS), tok(D),
            pl.BlockSpec((nsub, 1, 6 * D), lambda i, e: (i, 0, 0)),
            pl.BlockSpec((1, 1, D, D_EXPERT), lambda i, e: (l, e, 0, 0)),
            pl.BlockSpec((1, 1, D, D_EXPERT), lambda i, e: (l, e, 0, 0)),
            pl.BlockSpec((1, 1, D_EXPERT, D), lambda i, e: (l, e, 0, 0)),
            lay3(w["ln2_g"]), lay3(w["ln2_b"]),
        ],
        out_specs=tok(D),
        out_shape=jax.ShapeDtypeStruct((n, D), F32),
        scratch_shapes=[pltpu.VMEM((MOE_TILE, D), F32)],
        compiler_params=_cparams(("parallel", "arbitrary")),
        name="moe",
    )(f, comb, x1, mod_tiles, w["w_gate"], w["w_up"], w["w_down"], w["ln2_g"], w["ln2_b"])


def _rope_tables(seq, ctx_len):
    rows = seq // GRID_W
    row = jnp.repeat(jnp.arange(rows, dtype=F32), GRID_W)
    col = jnp.tile(jnp.arange(GRID_W, dtype=F32), rows)
    n_freq = QK_ROPE // 4
    inv = ROPE_BASE ** (-jnp.arange(n_freq, dtype=F32) / n_freq)
    ang = jnp.stack([row[:, None] * inv, col[:, None] * inv], axis=1)
    cos, sin = jnp.cos(ang), jnp.sin(ang)
    cos32 = jnp.broadcast_to(cos[:, :, None, :], (seq, 2, 2, n_freq)).reshape(seq, QK_ROPE)
    sign = jnp.array([-1.0, 1.0], F32)[None, None, :, None]
    sin32 = (jnp.broadcast_to(sin[:, :, None, :], (seq, 2, 2, n_freq)) * sign).reshape(seq, QK_ROPE)
    t = ctx_len + seq
    cos_t = jnp.ones((t, HEAD_PAD), F32).at[ctx_len:, ROPE_LANE0:ROPE_LANE0 + QK_ROPE].set(cos32)
    sin_t = jnp.zeros((t, HEAD_PAD), F32).at[ctx_len:, ROPE_LANE0:ROPE_LANE0 + QK_ROPE].set(sin32)
    return cos_t, sin_t


def _swap_perm():
    n_freq = QK_ROPE // 4
    r = np.arange(QK_ROPE)
    axis, half, f = r // (2 * n_freq), (r // n_freq) % 2, r % n_freq
    return axis * 2 * n_freq + (1 - half) * n_freq + f


def _pack_weights(w_in, w_uq, w_ukv, w_s, b_s, w_r, b_r, w_i, b_i, lru_lambda):
    depth = w_in.shape[0]
    perm = _swap_perm()
    dq = QK_NOPE + QK_ROPE
    wq = w_uq.reshape(depth, Q_RANK, HEADS, dq)
    zq = jnp.zeros((depth, Q_RANK, HEADS, HEAD_PAD - dq), w_uq.dtype)
    main = jnp.concatenate([wq, zq], axis=-1)
    swp = jnp.concatenate([jnp.zeros_like(wq[..., :QK_NOPE]), wq[..., QK_NOPE + perm], zq], axis=-1)
    wuq = jnp.concatenate([main.reshape(depth, Q_RANK, -1), swp.reshape(depth, Q_RANK, -1)], axis=-1)
    wkv = w_ukv.reshape(depth, KV_RANK, HEADS, QK_NOPE + V_DIM)
    wuk = jnp.concatenate([wkv[..., :QK_NOPE], jnp.zeros((depth, KV_RANK, HEADS, HEAD_PAD - QK_NOPE), w_ukv.dtype)],
                          axis=-1).reshape(depth, KV_RANK, -1)
    wuv = wkv[..., QK_NOPE:].reshape(depth, KV_RANK, -1)
    wkr = w_in[:, :, OFF_KR:OFF_KR + QK_ROPE]
    zl = jnp.zeros((depth, D, ROPE_LANE0), w_in.dtype)
    zr = jnp.zeros((depth, D, HEAD_PAD - ROPE_LANE0 - QK_ROPE), w_in.dtype)
    wb = jnp.concatenate([w_in[:, :, OFF_CQ:OFF_KR], zl, wkr, zr, zl, wkr[:, :, perm], zr], axis=-1)
    eye = jnp.eye(C_BLOCKS, dtype=w_r.dtype)

    def dense(wblk):
        full = jnp.einsum("ldhij,hk->ldhikj", wblk, eye)
        return full.reshape(depth, 2, C_WIDTH, C_WIDTH)

    wri = jnp.concatenate([dense(w_r), dense(w_i)], axis=-1).reshape(depth * 2, C_WIDTH, 2 * C_WIDTH)
    bri = jnp.concatenate([b_r, b_i], axis=-1).reshape(depth * 2, 1, 2 * C_WIDTH)
    gw = A_WIDTH // A_GROUPS
    bs_full = jnp.broadcast_to(jnp.swapaxes(b_s, 1, 2)[:, :, :, None],
                               (depth, CHUNK, A_GROUPS, gw)).reshape(depth, CHUNK, A_WIDTH)
    return {
        "wa": w_in[:, :, OFF_AU:OFF_CQ].astype(BF16),
        "wb": wb.astype(BF16),
        "wc": w_in[:, :, OFF_CX:OFF_GATE].astype(BF16),
        "wg": w_in[:, :, OFF_GATE:].astype(BF16),
        "wuq": wuq.astype(BF16), "wuk": wuk.astype(BF16), "wuv": wuv.astype(BF16),
        "w_s": w_s.astype(BF16), "bs_full": bs_full,
        "wri": wri.astype(BF16), "bri": bri,
        "lam": lru_lambda.reshape(depth * 2, 1, C_WIDTH),
    }


def kernel(x, c, ctx, c_ctx, w_ada, b_ada, w_in, a_ln_g, a_ln_b, w_s, b_s, q_norm_g, w_uq,
           kv_norm_g, w_ukv, conv_w, conv_b, w_r, b_r, w_i, b_i, lru_lambda, w_br, w_o,
           ln1_g, ln1_b, w_router, router_bias, w_gate, w_up, w_down, ln2_g, ln2_b):
    depth = w_in.shape[0]
    bsz, seq, _ = x.shape
    ctx_len = ctx.shape[1]
    assert ctx_len == TILE and seq % TILE == 0 and bsz == SUBLANES
    t = ctx_len + seq
    nct = t // TILE
    n = bsz * t
    assert n % MOE_TILE == 0
    alpha = (2.0 * depth) ** 0.25

    row3 = lambda a: a.reshape(depth, 1, a.shape[-1])
    w = _pack_weights(w_in, w_uq, w_ukv, w_s, b_s, w_r, b_r, w_i, b_i, lru_lambda)
    w.update({
        "a_ln_g": row3(a_ln_g), "a_ln_b": row3(a_ln_b),
        "q_norm_g": row3(q_norm_g), "kv_norm_g": row3(kv_norm_g),
        "conv_w": conv_w, "conv_b": row3(conv_b),
        "w_br": w_br.astype(BF16), "w_o": w_o.astype(BF16),
        "ln1_g": row3(ln1_g), "ln1_b": row3(ln1_b), "ln2_g": row3(ln2_g), "ln2_b": row3(ln2_b),
        "w_gate": w_gate.astype(BF16), "w_up": w_up.astype(BF16), "w_down": w_down.astype(BF16),
    })
    cos_t, sin_t = _rope_tables(seq, ctx_len)
    wrt = w_router.T
    rbias = router_bias.reshape(N_EXPERTS, 1)

    cond_rows = jnp.zeros((2 * SUBLANES, D), F32).at[:bsz].set(c).at[bsz].set(c_ctx)
    mod = _mod_call(cond_rows, w_ada, b_ada)
    tile_row = np.array([bsz if j == 0 else b for b in range(bsz) for j in range(nct)], np.int32)
    mod_tiles = mod[:, tile_row, :].reshape(depth, n // TILE, 1, 6 * D)

    xs = jnp.concatenate([ctx, x], axis=1).reshape(n, D)
    for l in range(depth):
        mt = mod_tiles[l]
        h, ya, q, k, v, xc, cg = _proj_call(l, xs, mt, cos_t, sin_t, w, nct)
        yb = _attn_call(q, k, v, bsz, nct)
        rec = _scan_call(l, xc.reshape(bsz, t, C_WIDTH), w, nct).reshape(2, n, C_WIDTH)
        x1, f, comb_t = _merge_call(l, xs, h, ya, yb, cg, rec, mt, w, wrt, rbias, alpha)
        xs = _moe_call(l, f, comb_t.T, x1, mt, w, alpha)
    return xs.reshape(bsz, t, D)[:, ctx_len:, :]
```

```python
import functools

import numpy as np
import jax
import jax.numpy as jnp
from jax import lax
from jax.experimental import pallas as pl
from jax.experimental.pallas import tpu as pltpu

D = 1024
GRID_W = 64
A_WIDTH = 512
A_GROUPS = 4
CHUNK = 128
HEADS = 8
QK_NOPE = 64
QK_ROPE = 32
V_DIM = 64
Q_RANK = 256
KV_RANK = 128
ROPE_BASE = 10000.0
C_WIDTH = 512
C_BLOCKS = 8
C_BLOCK_W = C_WIDTH // C_BLOCKS
CONV_W = 4
CONV_LEFT = 2
LRU_C = 8.0
N_BRANCH = 3
BRANCH_W = 512
OFF_AU = 0
OFF_AV = OFF_AU + A_WIDTH
OFF_CQ = OFF_AV + A_WIDTH
OFF_CKV = OFF_CQ + Q_RANK
OFF_KR = OFF_CKV + KV_RANK
OFF_CX = OFF_KR + QK_ROPE
OFF_CG = OFF_CX + C_WIDTH
OFF_GATE = OFF_CG + C_WIDTH
N_EXPERTS = 16
N_GROUPS = 4
EXPERTS_PER_GROUP = N_EXPERTS // N_GROUPS
D_EXPERT = 512
ROUTED_SCALE = 2.5
EPS = 1e-6

LANES = 128
SUBLANES = 8
TILE = 256
HEAD_PAD = 128
ROPE_LANE0 = QK_NOPE
MOE_TILE = 256
VMEM_LIMIT = 56 * 1024 * 1024

BF16 = jnp.bfloat16
F32 = jnp.float32


def _cparams(sem):
    return pltpu.CompilerParams(dimension_semantics=sem, vmem_limit_bytes=VMEM_LIMIT)


def _dot(a, b):
    return jnp.dot(a, b, preferred_element_type=F32)


def _layer_norm(v, g, b):
    mu = jnp.mean(v, axis=-1, keepdims=True)
    c = v - mu
    var = jnp.mean(c * c, axis=-1, keepdims=True)
    return (c * lax.rsqrt(var + EPS)) * g + b


def _rms_norm(v, g):
    ms = jnp.mean(v * v, axis=-1, keepdims=True)
    return (v * lax.rsqrt(ms + EPS)) * g


def _mod_kernel(c_ref, w_ref, b_ref, o_ref):
    c = c_ref[...]
    cond = (c * jax.nn.sigmoid(c)).astype(BF16)
    o_ref[0] = _dot(cond, w_ref[0].astype(BF16)) + b_ref[0]


def _mod_call(cond_rows, w_ada, b_ada):
    depth = w_ada.shape[0]
    rows = cond_rows.shape[0]
    ncol = 6 * D
    bn = 1536
    return pl.pallas_call(
        _mod_kernel,
        grid=(depth, ncol // bn),
        in_specs=[
            pl.BlockSpec((rows, D), lambda l, j: (0, 0)),
            pl.BlockSpec((1, D, bn), lambda l, j: (l, 0, j)),
            pl.BlockSpec((1, 1, bn), lambda l, j: (l, 0, j)),
        ],
        out_specs=pl.BlockSpec((1, rows, bn), lambda l, j: (l, 0, j)),
        out_shape=jax.ShapeDtypeStruct((depth, rows, ncol), F32),
        compiler_params=_cparams(("parallel", "parallel")),
        name="mod",
    )(cond_rows, w_ada, b_ada.reshape(depth, 1, ncol))


def _proj_kernel(x_ref, mod_ref, cos_ref, sin_ref, wa_ref, wb_ref, wc_ref,
                 alng_ref, alnb_ref, ws_ref, bs_ref, qg_ref, wuq_ref, kvg_ref,
                 wuk_ref, wuv_ref,
                 h_ref, ya_ref, q_ref, k_ref, v_ref, xc_ref, cg_ref):
    x = x_ref[...]
    m = mod_ref[0]
    sh1 = m[:, 0:D]
    sc1 = m[:, D:2 * D]
    h = (x * (1.0 + sc1) + sh1).astype(BF16)
    h_ref[...] = h

    uv = _dot(h, wa_ref[0])
    u = jax.nn.gelu(uv[:, :A_WIDTH])
    v = _layer_norm(jax.nn.gelu(uv[:, A_WIDTH:]), alng_ref[0], alnb_ref[0]).astype(BF16)
    gw = A_WIDTH // A_GROUPS
    for c in range(TILE // CHUNK):
        rows = slice(c * CHUNK, (c + 1) * CHUNK)
        for g in range(A_GROUPS):
            cols = slice(g * gw, (g + 1) * gw)
            mixed = _dot(ws_ref[0, g], v[rows, cols]) + bs_ref[0, :, cols]
            ya_ref[rows, cols] = (u[rows, cols] * mixed).astype(BF16)

    pb = _dot(h, wb_ref[0])
    cos_t = cos_ref[...]
    sin_t = sin_ref[...]
    cqn = _rms_norm(pb[:, :Q_RANK], qg_ref[0]).astype(BF16)
    qq = _dot(cqn, wuq_ref[0])
    hw = HEADS * HEAD_PAD
    for hd in range(HEADS):
        a = slice(hd * HEAD_PAD, (hd + 1) * HEAD_PAD)
        b = slice(hw + hd * HEAD_PAD, hw + (hd + 1) * HEAD_PAD)
        q_ref[:, a] = (qq[:, a] * cos_t + qq[:, b] * sin_t).astype(BF16)
    ckvn = _rms_norm(pb[:, Q_RANK:Q_RANK + KV_RANK], kvg_ref[0]).astype(BF16)
    o_kr = Q_RANK + KV_RANK
    kr = pb[:, o_kr:o_kr + HEAD_PAD] * cos_t + pb[:, o_kr + HEAD_PAD:o_kr + 2 * HEAD_PAD] * sin_t
    kk = _dot(ckvn, wuk_ref[0])
    for hd in range(HEADS):
        a = slice(hd * HEAD_PAD, (hd + 1) * HEAD_PAD)
        k_ref[:, a] = (kk[:, a] + kr).astype(BF16)
    v_ref[...] = _dot(ckvn, wuv_ref[0]).astype(BF16)

    pc = _dot(h, wc_ref[0])
    xc_ref[...] = pc[:, :C_WIDTH]
    cg_ref[...] = jax.nn.gelu(pc[:, C_WIDTH:])


def _proj_call(l, x, mod_tiles, cos_t, sin_t, w, nct):
    n = x.shape[0]
    nt = n // TILE
    tok = lambda width: pl.BlockSpec((TILE, width), lambda i: (i, 0))
    lay3 = lambda a: pl.BlockSpec((1,) + a.shape[1:], lambda i: (l, 0, 0))
    lay4 = lambda a: pl.BlockSpec((1,) + a.shape[1:], lambda i: (l, 0, 0, 0))
    pos = pl.BlockSpec((TILE, HEAD_PAD), lambda i: (i % nct, 0))
    outs = [
        (D, BF16), (A_WIDTH, BF16), (HEADS * HEAD_PAD, BF16), (HEADS * HEAD_PAD, BF16),
        (HEADS * V_DIM, BF16), (C_WIDTH, F32), (C_WIDTH, F32),
    ]
    return pl.pallas_call(
        _proj_kernel,
        grid=(nt,),
        in_specs=[
            tok(D),
            pl.BlockSpec((1, 1, 6 * D), lambda i: (i, 0, 0)),
            pos, pos,
            lay3(w["wa"]), lay3(w["wb"]), lay3(w["wc"]),
            lay3(w["a_ln_g"]), lay3(w["a_ln_b"]), lay4(w["w_s"]), lay3(w["bs_full"]),
            lay3(w["q_norm_g"]), lay3(w["wuq"]), lay3(w["kv_norm_g"]),
            lay3(w["wuk"]), lay3(w["wuv"]),
        ],
        out_specs=[tok(wd) for wd, _ in outs],
        out_shape=[jax.ShapeDtypeStruct((n, wd), dt) for wd, dt in outs],
        compiler_params=_cparams(("parallel",)),
        name="proj",
    )(x, mod_tiles, cos_t, sin_t, w["wa"], w["wb"], w["wc"], w["a_ln_g"], w["a_ln_b"],
      w["w_s"], w["bs_full"], w["q_norm_g"], w["wuq"], w["kv_norm_g"], w["wuk"], w["wuv"])


def _attn_kernel(q_ref, k_ref, v_ref, o_ref):
    j = pl.program_id(2)
    scale = float(QK_NOPE + QK_ROPE) ** -0.5
    lane = lax.broadcasted_iota(jnp.int32, (TILE, 2 * V_DIM), 1)

    def attend(nk):
        vv = v_ref[:nk, :]
        outs = []
        for hd in range(2):
            cols = slice(hd * HEAD_PAD, (hd + 1) * HEAD_PAD)
            s = lax.dot_general(q_ref[:, cols], k_ref[:nk, cols],
                                (((1,), (1,)), ((), ())),
                                preferred_element_type=F32) * scale
            mx = jnp.max(s, axis=-1, keepdims=True)
            p = jnp.exp(s - mx)
            den = jnp.sum(p, axis=-1, keepdims=True)
            outs.append(_dot(p.astype(BF16), vv) / den)
        o_ref[...] = jnp.where(lane < V_DIM, outs[0], outs[1]).astype(BF16)

    @pl.when(j == 0)
    def _():
        attend(TILE)

    @pl.when(j > 0)
    def _():
        attend(k_ref.shape[0])


def _attn_call(q, k, v, bsz, nct):
    n = q.shape[0]
    t = nct * TILE
    npair = HEADS // 2
    return pl.pallas_call(
        _attn_kernel,
        grid=(bsz, npair, nct),
        in_specs=[
            pl.BlockSpec((TILE, 2 * HEAD_PAD), lambda b, p, j: (b * nct + j, p)),
            pl.BlockSpec((t, 2 * HEAD_PAD), lambda b, p, j: (b, p)),
            pl.BlockSpec((t, 2 * V_DIM), lambda b, p, j: (b, p)),
        ],
        out_specs=pl.BlockSpec((TILE, 2 * V_DIM), lambda b, p, j: (b * nct + j, p)),
        out_shape=jax.ShapeDtypeStruct((n, HEADS * V_DIM), BF16),
        compiler_params=_cparams(("parallel", "parallel", "arbitrary")),
        name="attn",
    )(q, k, v)


def _scan_chunk(d, s, nct):
    return jnp.where(d == 0, s, jnp.where(s == 0, 0, nct - s))


def _scan_kernel(xc_ref, prev_ref, next_ref, cw_ref, cb_ref, wri_ref, bri_ref, lam_ref,
                 o_ref, xpad, a_s, bx_s, hs_s, carry, *, nct):
    d = pl.program_id(0)
    s = pl.program_id(1)
    chunk = _scan_chunk(d, s, nct)
    bsz = xc_ref.shape[0]
    halo = SUBLANES
    has_prev = jnp.logical_and(chunk != 0, chunk != 1)
    has_next = jnp.logical_and(chunk != 0, chunk != nct - 1)
    xpad[:, halo:halo + TILE, :] = xc_ref[...]
    xpad[:, 0:halo, :] = jnp.where(has_prev, prev_ref[...], 0.0)
    xpad[:, halo + TILE:halo + TILE + halo, :] = jnp.where(has_next, next_ref[...], 0.0)
    xconv = cb_ref[...]
    for kk in range(CONV_W):
        off = halo - CONV_LEFT + kk
        xconv = xconv + xpad[:, off:off + TILE, :] * cw_ref[0, kk:kk + 1, :]
    x2 = xconv.reshape(bsz * TILE, C_WIDTH)
    ri = _dot(x2.astype(BF16), wri_ref[0]) + bri_ref[0]
    r = jax.nn.sigmoid(ri[:, :C_WIDTH])
    ig = jax.nn.sigmoid(ri[:, C_WIDTH:])
    lam = lam_ref[0]
    log_sig = jnp.minimum(lam, 0.0) - jnp.log1p(jnp.exp(-jnp.abs(lam)))
    log_a = LRU_C * r * log_sig
    a = jnp.exp(log_a)
    bx = jnp.sqrt(-jnp.tanh(log_a) * (a * a + 1.0)) * (ig * x2)
    nlb = C_WIDTH // LANES
    for cb in range(nlb):
        cols = slice(cb * LANES, (cb + 1) * LANES)
        a_s[cb] = a[:, cols]
        bx_s[cb] = bx[:, cols]

    @pl.when(s == 0)
    def _():
        carry[...] = jnp.zeros_like(carry)

    def step(t, hprev):
        tt = jnp.where(d == 0, t, TILE - 1 - t)
        idx = pl.ds(tt, bsz, stride=TILE)
        hnew = []
        for cb in range(nlb):
            hc = a_s[cb, idx, :] * hprev[cb] + bx_s[cb, idx, :]
            hs_s[cb, idx, :] = hc
            hnew.append(hc)
        return tuple(hnew)

    h0 = tuple(carry[cb] for cb in range(nlb))
    hfin = lax.fori_loop(0, TILE, step, h0, unroll=8)
    for cb in range(nlb):
        carry[cb] = hfin[cb]
        o_ref[0, :, :, cb * LANES:(cb + 1) * LANES] = hs_s[cb].reshape(bsz, TILE, LANES)


def _scan_call(l, xc3, w, nct):
    bsz, t, _ = xc3.shape
    hb = TILE // SUBLANES
    nhb = t // SUBLANES
    cur = lambda d, s: (0, _scan_chunk(d, s, nct), 0)
    prv = lambda d, s: (0, jnp.maximum(_scan_chunk(d, s, nct) * hb - 1, 0), 0)
    nxt = lambda d, s: (0, jnp.minimum((_scan_chunk(d, s, nct) + 1) * hb, nhb - 1), 0)
    return pl.pallas_call(
        functools.partial(_scan_kernel, nct=nct),
        grid=(2, nct),
        in_specs=[
            pl.BlockSpec((bsz, TILE, C_WIDTH), cur),
            pl.BlockSpec((bsz, SUBLANES, C_WIDTH), prv),
            pl.BlockSpec((bsz, SUBLANES, C_WIDTH), nxt),
            pl.BlockSpec((1, CONV_W, C_WIDTH), lambda d, s: (l, 0, 0)),
            pl.BlockSpec((1, 1, C_WIDTH), lambda d, s: (l, 0, 0)),
            pl.BlockSpec((1, C_WIDTH, 2 * C_WIDTH), lambda d, s: (2 * l + d, 0, 0)),
            pl.BlockSpec((1, 1, 2 * C_WIDTH), lambda d, s: (2 * l + d, 0, 0)),
            pl.BlockSpec((1, 1, C_WIDTH), lambda d, s: (2 * l + d, 0, 0)),
        ],
        out_specs=pl.BlockSpec((1, bsz, TILE, C_WIDTH),
                               lambda d, s: (d, 0, _scan_chunk(d, s, nct), 0)),
        out_shape=jax.ShapeDtypeStruct((2, bsz, t, C_WIDTH), F32),
        scratch_shapes=[
            pltpu.VMEM((bsz, TILE + 2 * SUBLANES, C_WIDTH), F32),
            pltpu.VMEM((C_WIDTH // LANES, bsz * TILE, LANES), F32),
            pltpu.VMEM((C_WIDTH // LANES, bsz * TILE, LANES), F32),
            pltpu.VMEM((C_WIDTH // LANES, bsz * TILE, LANES), F32),
            pltpu.VMEM((C_WIDTH // LANES, bsz, LANES), F32),
        ],
        compiler_params=_cparams(("arbitrary", "arbitrary")),
        name="scan",
    )(xc3, xc3, xc3, w["conv_w"], w["conv_b"], w["wri"], w["bri"], w["lam"])


def _top2_sum(v):
    hi1, lo1 = jnp.maximum(v[0], v[1]), jnp.minimum(v[0], v[1])
    hi2, lo2 = jnp.maximum(v[2], v[3]), jnp.minimum(v[2], v[3])
    top = jnp.maximum(hi1, hi2)
    second = jnp.maximum(jnp.minimum(hi1, hi2), jnp.maximum(lo1, lo2))
    return top + second


def _route(scores, sel):
    gs = [_top2_sum(sel[g * EXPERTS_PER_GROUP:(g + 1) * EXPERTS_PER_GROUP]) for g in range(N_GROUPS)]
    best = jnp.zeros_like(gs[0], dtype=jnp.int32)
    best_v = gs[0]
    for g in range(1, N_GROUPS):
        upd = gs[g] > best_v
        best = jnp.where(upd, g, best)
        best_v = jnp.where(upd, gs[g], best_v)

    def pick(rows, k):
        out = rows[k]
        for g in range(1, N_GROUPS):
            out = jnp.where(best == g, rows[g * EXPERTS_PER_GROUP + k], out)
        return out

    in_sel = [pick(sel, k) for k in range(EXPERTS_PER_GROUP)]
    in_sc = [pick(scores, k) for k in range(EXPERTS_PER_GROUP)]
    neg = jnp.full_like(in_sel[0], -jnp.inf)

    def argmax4(vals):
        idx = jnp.zeros_like(best)
        val = vals[0]
        for k in range(1, EXPERTS_PER_GROUP):
            upd = vals[k] > val
            idx = jnp.where(upd, k, idx)
            val = jnp.where(upd, vals[k], val)
        return idx

    i1 = argmax4(in_sel)
    i2 = argmax4([jnp.where(i1 == k, neg, in_sel[k]) for k in range(EXPERTS_PER_GROUP)])

    def take(vals, idx):
        out = vals[0]
        for k in range(1, EXPERTS_PER_GROUP):
            out = jnp.where(idx == k, vals[k], out)
        return out

    w1 = take(in_sc, i1)
    w2 = take(in_sc, i2)
    den = w1 + w2
    return (best * EXPERTS_PER_GROUP + i1, best * EXPERTS_PER_GROUP + i2,
            ROUTED_SCALE * w1 / den, ROUTED_SCALE * w2 / den)


def _merge_kernel(x_ref, h_ref, ya_ref, yb_ref, cg_ref, rf_ref, rb_ref, mod_ref,
                  wg_ref, wbr_ref, wo_ref, lng_ref, lnb_ref, wrt_ref, rbias_ref,
                  x1_ref, f_ref, re_ref, rw_ref, rr_ref, *, alpha):
    m = mod_ref[0]
    g1 = m[:, 2 * D:3 * D]
    sh2 = m[:, 3 * D:4 * D]
    sc2 = m[:, 4 * D:5 * D]
    h = h_ref[...]
    yc = (cg_ref[...] * (rf_ref[0] + rb_ref[0])).astype(BF16)
    ys = (ya_ref[...], yb_ref[...], yc)
    mix = None
    for kb in range(N_BRANCH):
        gate = jax.nn.sigmoid(_dot(h, wg_ref[0, :, kb * D:(kb + 1) * D]))
        term = gate * _dot(ys[kb], wbr_ref[0, kb])
        mix = term if mix is None else mix + term
    y = _dot(mix.astype(BF16), wo_ref[0])
    x1 = _layer_norm(alpha * x_ref[...] + g1 * y, lng_ref[0], lnb_ref[0])
    x1_ref[...] = x1
    f = x1 * (1.0 + sc2) + sh2
    f_ref[...] = f

    logits = lax.dot_general(wrt_ref[...], f, (((1,), (1,)), ((), ())),
                             precision=lax.Precision.HIGHEST, preferred_element_type=F32)
    sc = jax.nn.sigmoid(logits)
    sl = sc + rbias_ref[...]
    scores = [sc[e:e + 1, :] for e in range(N_EXPERTS)]
    sel = [sl[e:e + 1, :] for e in range(N_EXPERTS)]
    e1, e2, w1, w2 = _route(scores, sel)
    re_ref[...] = jnp.concatenate([e1, e2], axis=0)
    rw_ref[...] = jnp.concatenate([w1, w2], axis=0)
    row_i = lax.broadcasted_iota(jnp.int32, (TILE, TILE), 0)
    col_i = lax.broadcasted_iota(jnp.int32, (TILE, TILE), 1)
    tri = jnp.where(row_i <= col_i, 1.0, 0.0).astype(BF16)
    ranks = []
    for ek in (e1, e2):
        onehot = jnp.concatenate([jnp.where(ek == e, 1.0, 0.0) for e in range(N_EXPERTS)], axis=0)
        prefix = _dot(onehot.astype(BF16), tri)
        ranks.append(jnp.sum(onehot * prefix, axis=0, keepdims=True) - 1.0)
    rr_ref[...] = jnp.concatenate(ranks, axis=0).astype(jnp.int32)


def _merge_call(l, x, h, ya, yb, cg, rec, mod_tiles, w, wrt, rbias, alpha):
    n = x.shape[0]
    nt = n // TILE
    tok = lambda width: pl.BlockSpec((TILE, width), lambda i: (i, 0))
    lay3 = lambda a: pl.BlockSpec((1,) + a.shape[1:], lambda i: (l, 0, 0))
    lay4 = lambda a: pl.BlockSpec((1,) + a.shape[1:], lambda i: (l, 0, 0, 0))
    return pl.pallas_call(
        functools.partial(_merge_kernel, alpha=alpha),
        grid=(nt,),
        in_specs=[
            tok(D), tok(D), tok(A_WIDTH), tok(HEADS * V_DIM), tok(C_WIDTH),
            pl.BlockSpec((1, TILE, C_WIDTH), lambda i: (0, i, 0)),
            pl.BlockSpec((1, TILE, C_WIDTH), lambda i: (1, i, 0)),
            pl.BlockSpec((1, 1, 6 * D), lambda i: (i, 0, 0)),
            lay3(w["wg"]), lay4(w["w_br"]), lay3(w["w_o"]), lay3(w["ln1_g"]), lay3(w["ln1_b"]),
            pl.BlockSpec(wrt.shape, lambda i: (0, 0)),
            pl.BlockSpec(rbias.shape, lambda i: (0, 0)),
        ],
        out_specs=[tok(D), tok(D)] + [pl.BlockSpec((2, TILE), lambda i: (0, i))] * 3,
        out_shape=[jax.ShapeDtypeStruct((n, D), F32), jax.ShapeDtypeStruct((n, D), F32),
                   jax.ShapeDtypeStruct((2, n), jnp.int32), jax.ShapeDtypeStruct((2, n), F32),
                   jax.ShapeDtypeStruct((2, n), jnp.int32)],
        compiler_params=_cparams(("parallel",)),
        name="merge",
    )(x, h, ya, yb, cg, rec, rec, mod_tiles, w["wg"], w["w_br"], w["w_o"], w["ln1_g"], w["ln1_b"],
      wrt, rbias)


def _row_gather_copy(src_hbm, row, dst, dst_row, sem):
    return pltpu.make_async_copy(src_hbm.at[pl.ds(row, 1), :], dst.at[pl.ds(dst_row, 1), :], sem)


def _tile_wait(src_hbm, dst, sem):
    pltpu.make_async_copy(src_hbm.at[pl.ds(0, dst.shape[0]), :], dst, sem).wait()


def _expert_kernel(src_ref, texp_ref, nvalid_ref, f_hbm, wgt_ref, wup_ref, wdn_ref,
                   o_ref, buf, sem):
    i = pl.program_id(0)
    nvalid = nvalid_ref[0]
    slot = i % 2

    def issue(tile, sl):
        base = tile * MOE_TILE
        for r in range(MOE_TILE):
            _row_gather_copy(f_hbm, src_ref[base + r], buf.at[sl], r, sem.at[sl]).start()

    @pl.when(jnp.logical_and(i == 0, nvalid > 0))
    def _():
        issue(0, 0)

    @pl.when(i + 1 < nvalid)
    def _():
        issue(i + 1, 1 - slot)

    @pl.when(i < nvalid)
    def _():
        _tile_wait(f_hbm, buf.at[slot], sem.at[slot])
        xin = buf[slot].astype(BF16)
        t = (jax.nn.silu(_dot(xin, wgt_ref[0, 0])) * _dot(xin, wup_ref[0, 0])).astype(BF16)
        o_ref[...] = _dot(t, wdn_ref[0, 0])

    @pl.when(i >= nvalid)
    def _():
        o_ref[...] = jnp.zeros_like(o_ref)


def _expert_call(l, f, src_tok, tile_expert, nvalid, w):
    ntile = tile_expert.shape[0]
    wspec = lambda a, b: pl.BlockSpec((1, 1, a, b), lambda i, src, texp, nv: (l, texp[i], 0, 0))
    return pl.pallas_call(
        _expert_kernel,
        grid_spec=pltpu.PrefetchScalarGridSpec(
            num_scalar_prefetch=3,
            grid=(ntile,),
            in_specs=[pl.BlockSpec(memory_space=pl.ANY),
                      wspec(D, D_EXPERT), wspec(D, D_EXPERT), wspec(D_EXPERT, D)],
            out_specs=pl.BlockSpec((MOE_TILE, D), lambda i, src, texp, nv: (i, 0)),
            scratch_shapes=[pltpu.VMEM((2, MOE_TILE, D), F32), pltpu.SemaphoreType.DMA((2,))],
        ),
        out_shape=jax.ShapeDtypeStruct((ntile * MOE_TILE, D), F32),
        compiler_params=_cparams(("arbitrary",)),
        name="expert",
    )(src_tok, tile_expert, nvalid, f, w["w_gate"], w["w_up"], w["w_down"])


def _combine_kernel(pos_ref, ys_hbm, rw_ref, x1_ref, mod_ref, lng_ref, lnb_ref, o_ref, buf, sem,
                    *, alpha, n):
    i = pl.program_id(0)
    slot = i % 2

    def issue(tile, sl):
        for k in range(2):
            base = k * n + tile * TILE
            for r in range(TILE):
                _row_gather_copy(ys_hbm, pos_ref[base + r], buf.at[sl, k], r, sem.at[sl]).start()

    @pl.when(i == 0)
    def _():
        issue(0, 0)

    @pl.when(i + 1 < pl.num_programs(0))
    def _():
        issue(i + 1, 1 - slot)

    for k in range(2):
        _tile_wait(ys_hbm, buf.at[slot, k], sem.at[slot])
    rw = rw_ref[...]
    fo = rw[:, 0:1] * buf[slot, 0] + rw[:, 1:2] * buf[slot, 1]
    g2 = mod_ref[0, :, 5 * D:6 * D]
    o_ref[...] = _layer_norm(alpha * x1_ref[...] + g2 * fo, lng_ref[0], lnb_ref[0])


def _combine_call(l, pos, ys, rw_t, x1, mod_tiles, w, alpha):
    n = x1.shape[0]
    nt = n // TILE
    tok = lambda width: pl.BlockSpec((TILE, width), lambda i, p: (i, 0))
    lay3 = lambda a: pl.BlockSpec((1,) + a.shape[1:], lambda i, p: (l, 0, 0))
    return pl.pallas_call(
        functools.partial(_combine_kernel, alpha=alpha, n=n),
        grid_spec=pltpu.PrefetchScalarGridSpec(
            num_scalar_prefetch=1,
            grid=(nt,),
            in_specs=[pl.BlockSpec(memory_space=pl.ANY), tok(2), tok(D),
                      pl.BlockSpec((1, 1, 6 * D), lambda i, p: (i, 0, 0)),
                      lay3(w["ln2_g"]), lay3(w["ln2_b"])],
            out_specs=tok(D),
            scratch_shapes=[pltpu.VMEM((2, 2, TILE, D), F32), pltpu.SemaphoreType.DMA((2,))],
        ),
        out_shape=jax.ShapeDtypeStruct((n, D), F32),
        compiler_params=_cparams(("arbitrary",)),
        name="combine",
    )(pos, ys, rw_t, x1, mod_tiles, w["ln2_g"], w["ln2_b"])


def _dispatch_plan(re, rr, nt):
    n = re.shape[1]
    onehot = re.reshape(2, nt, TILE, 1) == jnp.arange(N_EXPERTS, dtype=jnp.int32)
    counts = jnp.sum(onehot, axis=2, dtype=jnp.int32)
    flat = jnp.swapaxes(counts, 0, 1).reshape(nt * 2, N_EXPERTS)
    incl = jnp.cumsum(flat, axis=0)
    total = incl[-1]
    off = jnp.swapaxes((incl - flat).reshape(nt, 2, N_EXPERTS), 0, 1)
    padded = (total + MOE_TILE - 1) // MOE_TILE * MOE_TILE
    ends = jnp.cumsum(padded)
    base = ends - padded
    start = (base + off)[:, :, None, :]
    pos = jnp.sum(jnp.where(onehot, start, 0), axis=-1).reshape(2, n) + rr
    ntile = (2 * n) // MOE_TILE + N_EXPERTS
    tile_start = jnp.arange(ntile, dtype=jnp.int32) * MOE_TILE
    tile_expert = jnp.minimum(jnp.sum(ends[None, :] <= tile_start[:, None], axis=1), N_EXPERTS - 1)
    nvalid = (ends[-1] // MOE_TILE).reshape(1)
    tok = jnp.tile(jnp.arange(n, dtype=jnp.int32), 2)
    src_tok = jnp.zeros((ntile * MOE_TILE,), jnp.int32).at[pos.reshape(-1)].set(
        tok, unique_indices=True, mode="promise_in_bounds")
    return pos.reshape(-1).astype(jnp.int32), src_tok, tile_expert.astype(jnp.int32), nvalid.astype(jnp.int32)


def _rope_tables(seq, ctx_len):
    rows = seq // GRID_W
    row = jnp.repeat(jnp.arange(rows, dtype=F32), GRID_W)
    col = jnp.tile(jnp.arange(GRID_W, dtype=F32), rows)
    n_freq = QK_ROPE // 4
    inv = ROPE_BASE ** (-jnp.arange(n_freq, dtype=F32) / n_freq)
    ang = jnp.stack([row[:, None] * inv, col[:, None] * inv], axis=1)
    cos, sin = jnp.cos(ang), jnp.sin(ang)
    cos32 = jnp.broadcast_to(cos[:, :, None, :], (seq, 2, 2, n_freq)).reshape(seq, QK_ROPE)
    sign = jnp.array([-1.0, 1.0], F32)[None, None, :, None]
    sin32 = (jnp.broadcast_to(sin[:, :, None, :], (seq, 2, 2, n_freq)) * sign).reshape(seq, QK_ROPE)
    t = ctx_len + seq
    cos_t = jnp.ones((t, HEAD_PAD), F32).at[ctx_len:, ROPE_LANE0:ROPE_LANE0 + QK_ROPE].set(cos32)
    sin_t = jnp.zeros((t, HEAD_PAD), F32).at[ctx_len:, ROPE_LANE0:ROPE_LANE0 + QK_ROPE].set(sin32)
    return cos_t, sin_t


def _swap_perm():
    n_freq = QK_ROPE // 4
    r = np.arange(QK_ROPE)
    axis, half, f = r // (2 * n_freq), (r // n_freq) % 2, r % n_freq
    return axis * 2 * n_freq + (1 - half) * n_freq + f


def _pack_weights(w_in, w_uq, w_ukv, w_s, b_s, w_r, b_r, w_i, b_i, lru_lambda):
    depth = w_in.shape[0]
    perm = _swap_perm()
    dq = QK_NOPE + QK_ROPE
    wq = w_uq.reshape(depth, Q_RANK, HEADS, dq)
    zq = jnp.zeros((depth, Q_RANK, HEADS, HEAD_PAD - dq), w_uq.dtype)
    main = jnp.concatenate([wq, zq], axis=-1)
    swp = jnp.concatenate([jnp.zeros_like(wq[..., :QK_NOPE]), wq[..., QK_NOPE + perm], zq], axis=-1)
    wuq = jnp.concatenate([main.reshape(depth, Q_RANK, -1), swp.reshape(depth, Q_RANK, -1)], axis=-1)
    wkv = w_ukv.reshape(depth, KV_RANK, HEADS, QK_NOPE + V_DIM)
    wuk = jnp.concatenate([wkv[..., :QK_NOPE], jnp.zeros((depth, KV_RANK, HEADS, HEAD_PAD - QK_NOPE), w_ukv.dtype)],
                          axis=-1).reshape(depth, KV_RANK, -1)
    wuv = wkv[..., QK_NOPE:].reshape(depth, KV_RANK, -1)
    wkr = w_in[:, :, OFF_KR:OFF_KR + QK_ROPE]
    zl = jnp.zeros((depth, D, ROPE_LANE0), w_in.dtype)
    zr = jnp.zeros((depth, D, HEAD_PAD - ROPE_LANE0 - QK_ROPE), w_in.dtype)
    wb = jnp.concatenate([w_in[:, :, OFF_CQ:OFF_KR], zl, wkr, zr, zl, wkr[:, :, perm], zr], axis=-1)
    eye = jnp.eye(C_BLOCKS, dtype=w_r.dtype)

    def dense(wblk):
        full = jnp.einsum("ldhij,hk->ldhikj", wblk, eye)
        return full.reshape(depth, 2, C_WIDTH, C_WIDTH)

    wri = jnp.concatenate([dense(w_r), dense(w_i)], axis=-1).reshape(depth * 2, C_WIDTH, 2 * C_WIDTH)
    bri = jnp.concatenate([b_r, b_i], axis=-1).reshape(depth * 2, 1, 2 * C_WIDTH)
    gw = A_WIDTH // A_GROUPS
    bs_full = jnp.broadcast_to(jnp.swapaxes(b_s, 1, 2)[:, :, :, None],
                               (depth, CHUNK, A_GROUPS, gw)).reshape(depth, CHUNK, A_WIDTH)
    return {
        "wa": w_in[:, :, OFF_AU:OFF_CQ].astype(BF16),
        "wb": wb.astype(BF16),
        "wc": w_in[:, :, OFF_CX:OFF_GATE].astype(BF16),
        "wg": w_in[:, :, OFF_GATE:].astype(BF16),
        "wuq": wuq.astype(BF16), "wuk": wuk.astype(BF16), "wuv": wuv.astype(BF16),
        "w_s": w_s.astype(BF16), "bs_full": bs_full,
        "wri": wri.astype(BF16), "bri": bri,
        "lam": lru_lambda.reshape(depth * 2, 1, C_WIDTH),
    }


def kernel(x, c, ctx, c_ctx, w_ada, b_ada, w_in, a_ln_g, a_ln_b, w_s, b_s, q_norm_g, w_uq,
           kv_norm_g, w_ukv, conv_w, conv_b, w_r, b_r, w_i, b_i, lru_lambda, w_br, w_o,
           ln1_g, ln1_b, w_router, router_bias, w_gate, w_up, w_down, ln2_g, ln2_b):
    depth = w_in.shape[0]
    bsz, seq, _ = x.shape
    ctx_len = ctx.shape[1]
    assert ctx_len == TILE and seq % TILE == 0 and bsz == SUBLANES
    t = ctx_len + seq
    nct = t // TILE
    n = bsz * t
    assert n % MOE_TILE == 0
    alpha = (2.0 * depth) ** 0.25

    row3 = lambda a: a.reshape(depth, 1, a.shape[-1])
    w = _pack_weights(w_in, w_uq, w_ukv, w_s, b_s, w_r, b_r, w_i, b_i, lru_lambda)
    w.update({
        "a_ln_g": row3(a_ln_g), "a_ln_b": row3(a_ln_b),
        "q_norm_g": row3(q_norm_g), "kv_norm_g": row3(kv_norm_g),
        "conv_w": conv_w, "conv_b": row3(conv_b),
        "w_br": w_br.astype(BF16), "w_o": w_o.astype(BF16),
        "ln1_g": row3(ln1_g), "ln1_b": row3(ln1_b), "ln2_g": row3(ln2_g), "ln2_b": row3(ln2_b),
        "w_gate": w_gate.astype(BF16), "w_up": w_up.astype(BF16), "w_down": w_down.astype(BF16),
    })
    cos_t, sin_t = _rope_tables(seq, ctx_len)
    wrt = w_router.T
    rbias = router_bias.reshape(N_EXPERTS, 1)

    cond_rows = jnp.zeros((2 * SUBLANES, D), F32).at[:bsz].set(c).at[bsz].set(c_ctx)
    mod = _mod_call(cond_rows, w_ada, b_ada)
    tile_row = np.array([bsz if j == 0 else b for b in range(bsz) for j in range(nct)], np.int32)
    mod_tiles = mod[:, tile_row, :].reshape(depth, n // TILE, 1, 6 * D)

    xs = jnp.concatenate([ctx, x], axis=1).reshape(n, D)
    for l in range(depth):
        mt = mod_tiles[l]
        h, ya, q, k, v, xc, cg = _proj_call(l, xs, mt, cos_t, sin_t, w, nct)
        yb = _attn_call(q, k, v, bsz, nct)
        rec = _scan_call(l, xc.reshape(bsz, t, C_WIDTH), w, nct).reshape(2, n, C_WIDTH)
        x1, f, re, rw, rr = _merge_call(l, xs, h, ya, yb, cg, rec, mt, w, wrt, rbias, alpha)
        pos, src_tok, tile_expert, nvalid = _dispatch_plan(re, rr, n // TILE)
        ys = _expert_call(l, f, src_tok, tile_expert, nvalid, w)
        xs = _combine_call(l, pos, ys, rw.T, x1, mt, w, alpha)
    return xs.reshape(bsz, t, D)[:, ctx_len:, :]
```

```python
import functools

import numpy as np
import jax
import jax.numpy as jnp
from jax import lax
from jax.experimental import pallas as pl
from jax.experimental.pallas import tpu as pltpu

D = 1024
GRID_W = 64
A_WIDTH = 512
A_GROUPS = 4
CHUNK = 128
HEADS = 8
QK_NOPE = 64
QK_ROPE = 32
V_DIM = 64
Q_RANK = 256
KV_RANK = 128
ROPE_BASE = 10000.0
C_WIDTH = 512
C_BLOCKS = 8
C_BLOCK_W = C_WIDTH // C_BLOCKS
CONV_W = 4
CONV_LEFT = 2
LRU_C = 8.0
N_BRANCH = 3
BRANCH_W = 512
OFF_AU = 0
OFF_AV = OFF_AU + A_WIDTH
OFF_CQ = OFF_AV + A_WIDTH
OFF_CKV = OFF_CQ + Q_RANK
OFF_KR = OFF_CKV + KV_RANK
OFF_CX = OFF_KR + QK_ROPE
OFF_CG = OFF_CX + C_WIDTH
OFF_GATE = OFF_CG + C_WIDTH
N_EXPERTS = 16
N_GROUPS = 4
EXPERTS_PER_GROUP = N_EXPERTS // N_GROUPS
D_EXPERT = 512
ROUTED_SCALE = 2.5
EPS = 1e-6

LANES = 128
SUBLANES = 8
TILE = 256
HEAD_PAD = 128
ROPE_LANE0 = QK_NOPE
MOE_TILE = 256
ATT_Q = 128
ATT_K = 2 * LANES
MM_TILE = 512
VMEM_LIMIT = 56 * 1024 * 1024

BF16 = jnp.bfloat16
F32 = jnp.float32


def _cparams(sem):
    return pltpu.CompilerParams(dimension_semantics=sem, vmem_limit_bytes=VMEM_LIMIT)


def _dot(a, b):
    return jnp.dot(a, b, preferred_element_type=F32)


def _sigmoid(v):
    return 0.5 * jnp.tanh(0.5 * v) + 0.5


def _layer_norm(v, g, b):
    mu = jnp.mean(v, axis=-1, keepdims=True)
    c = v - mu
    var = jnp.mean(c * c, axis=-1, keepdims=True)
    return (c * lax.rsqrt(var + EPS)) * g + b


def _rms_norm(v, g):
    ms = jnp.mean(v * v, axis=-1, keepdims=True)
    return (v * lax.rsqrt(ms + EPS)) * g


def _mod_kernel(c_ref, w_ref, b_ref, o_ref):
    c = c_ref[...]
    cond = (c * jax.nn.sigmoid(c)).astype(BF16)
    o_ref[0] = _dot(cond, w_ref[0].astype(BF16)) + b_ref[0]


def _mod_call(cond_rows, w_ada, b_ada):
    depth = w_ada.shape[0]
    rows = cond_rows.shape[0]
    ncol = 6 * D
    bn = 1536
    return pl.pallas_call(
        _mod_kernel,
        grid=(depth, ncol // bn),
        in_specs=[
            pl.BlockSpec((rows, D), lambda l, j: (0, 0)),
            pl.BlockSpec((1, D, bn), lambda l, j: (l, 0, j)),
            pl.BlockSpec((1, 1, bn), lambda l, j: (l, 0, j)),
        ],
        out_specs=pl.BlockSpec((1, rows, bn), lambda l, j: (l, 0, j)),
        out_shape=jax.ShapeDtypeStruct((depth, rows, ncol), F32),
        compiler_params=_cparams(("parallel", "parallel")),
        name="mod",
    )(cond_rows, w_ada, b_ada.reshape(depth, 1, ncol))


def _proj_kernel(x_ref, mod_ref, cos0_ref, cos1_ref, sin0_ref, sin1_ref, wa_ref, wb_ref, wc_ref,
                 alng_ref, alnb_ref, ws_ref, bs_ref, qg_ref, wuq_ref, kvg_ref,
                 wuk_ref, wuv_ref,
                 h_ref, ya_ref, q_ref, k_ref, v_ref, xc_ref, cg_ref):
    halves = []
    for sub in range(MM_TILE // TILE):
        rows = slice(sub * TILE, (sub + 1) * TILE)
        m = mod_ref[sub]
        halves.append((x_ref[rows, :] * (1.0 + m[:, D:2 * D]) + m[:, 0:D]).astype(BF16))
    h = jnp.concatenate(halves, axis=0)
    h_ref[...] = h

    uv = _dot(h, wa_ref[0])
    u = jax.nn.gelu(uv[:, :A_WIDTH])
    v = _layer_norm(jax.nn.gelu(uv[:, A_WIDTH:]), alng_ref[0], alnb_ref[0]).astype(BF16)
    gw = A_WIDTH // A_GROUPS
    for c in range(MM_TILE // CHUNK):
        rows = slice(c * CHUNK, (c + 1) * CHUNK)
        for g in range(A_GROUPS):
            cols = slice(g * gw, (g + 1) * gw)
            mixed = _dot(ws_ref[0, g], v[rows, cols]) + bs_ref[0, :, cols]
            ya_ref[rows, cols] = (u[rows, cols] * mixed).astype(BF16)

    pb = _dot(h, wb_ref[0])
    cos_t = jnp.concatenate([cos0_ref[...], cos1_ref[...]], axis=0)
    sin_t = jnp.concatenate([sin0_ref[...], sin1_ref[...]], axis=0)
    cqn = _rms_norm(pb[:, :Q_RANK], qg_ref[0]).astype(BF16)
    qq = _dot(cqn, wuq_ref[0])
    hw = HEADS * HEAD_PAD
    for hd in range(HEADS):
        a = slice(hd * HEAD_PAD, (hd + 1) * HEAD_PAD)
        b = slice(hw + hd * HEAD_PAD, hw + (hd + 1) * HEAD_PAD)
        q_ref[:, a] = (qq[:, a] * cos_t + qq[:, b] * sin_t).astype(BF16)
    ckvn = _rms_norm(pb[:, Q_RANK:Q_RANK + KV_RANK], kvg_ref[0]).astype(BF16)
    o_kr = Q_RANK + KV_RANK
    kr = pb[:, o_kr:o_kr + HEAD_PAD] * cos_t + pb[:, o_kr + HEAD_PAD:o_kr + 2 * HEAD_PAD] * sin_t
    kk = _dot(ckvn, wuk_ref[0])
    for hd in range(HEADS):
        a = slice(hd * HEAD_PAD, (hd + 1) * HEAD_PAD)
        k_ref[:, a] = (kk[:, a] + kr).astype(BF16)
    v_ref[...] = _dot(ckvn, wuv_ref[0]).astype(BF16)

    pc = _dot(h, wc_ref[0])
    xc_ref[...] = pc[:, :C_WIDTH]
    cg_ref[...] = jax.nn.gelu(pc[:, C_WIDTH:])


def _const_spec(a, l):
    nz = (0,) * (a.ndim - 1)
    return pl.BlockSpec((1,) + a.shape[1:], lambda i: (l,) + nz, pipeline_mode=pl.Buffered(1))


def _proj_call(l, x, mod_tiles, cos_t, sin_t, w, nct):
    n = x.shape[0]
    nsub = MM_TILE // TILE
    assert nsub == 2
    tok = lambda width: pl.BlockSpec((MM_TILE, width), lambda i: (i, 0))
    pos0 = pl.BlockSpec((TILE, HEAD_PAD), lambda i: ((nsub * i) % nct, 0))
    pos1 = pl.BlockSpec((TILE, HEAD_PAD), lambda i: ((nsub * i + 1) % nct, 0))
    outs = [
        (D, BF16), (A_WIDTH, BF16), (HEADS * HEAD_PAD, BF16), (HEADS * HEAD_PAD, BF16),
        (HEADS * V_DIM, BF16), (C_WIDTH, F32), (C_WIDTH, F32),
    ]
    names = ["wa", "wb", "wc", "a_ln_g", "a_ln_b", "w_s", "bs_full", "q_norm_g", "wuq",
             "kv_norm_g", "wuk", "wuv"]
    return pl.pallas_call(
        _proj_kernel,
        grid=(n // MM_TILE,),
        in_specs=[
            tok(D),
            pl.BlockSpec((nsub, 1, 6 * D), lambda i: (i, 0, 0)),
            pos0, pos1, pos0, pos1,
        ] + [_const_spec(w[nm], l) for nm in names],
        out_specs=[tok(wd) for wd, _ in outs],
        out_shape=[jax.ShapeDtypeStruct((n, wd), dt) for wd, dt in outs],
        compiler_params=_cparams(("parallel",)),
        name="proj",
    )(x, mod_tiles, cos_t, cos_t, sin_t, sin_t, *[w[nm] for nm in names])


def _attn_kernel(q_ref, k_ref, v_ref, o_ref, s_buf, vext):
    j = pl.program_id(2)
    c2 = float(QK_NOPE + QK_ROPE) ** -0.5 * float(np.log2(np.e))
    nt_dims = (((1,), (1,)), ((), ()))
    nv = 2 * V_DIM

    @pl.when(j == 0)
    def _():
        vext[:, :nv] = v_ref[...]
        vext[:, nv:] = jnp.ones((vext.shape[0], vext.shape[1] - nv), BF16)

    def attend(nk):
        for qh in range(TILE // ATT_Q):
            rows = slice(qh * ATT_Q, (qh + 1) * ATT_Q)
            for hd in range(2):
                cols = slice(hd * HEAD_PAD, (hd + 1) * HEAD_PAD)
                q = q_ref[rows, cols]
                m = None
                for c in range(nk // ATT_K):
                    keys = slice(c * ATT_K, (c + 1) * ATT_K)
                    s = lax.dot_general(q, k_ref[keys, cols], nt_dims, preferred_element_type=F32)
                    s_buf[:, keys] = s
                    sm = jnp.maximum(s[:, :LANES], s[:, LANES:])
                    m = sm if m is None else jnp.maximum(m, sm)
                mx = jnp.max(m, axis=-1, keepdims=True)
                acc = None
                for c in range(nk // ATT_K):
                    keys = slice(c * ATT_K, (c + 1) * ATT_K)
                    p = jnp.exp2((s_buf[:, keys] - mx) * c2)
                    pv = _dot(p.astype(BF16), vext[keys, :])
                    acc = pv if acc is None else acc + pv
                out = (acc[:, :nv] / acc[:, nv:]).astype(BF16)
                vcols = slice(hd * V_DIM, (hd + 1) * V_DIM)
                o_ref[rows, vcols] = out[:, vcols]

    @pl.when(j == 0)
    def _():
        attend(TILE)

    @pl.when(j > 0)
    def _():
        attend(k_ref.shape[0])


def _attn_call(q, k, v, bsz, nct):
    n = q.shape[0]
    t = nct * TILE
    npair = HEADS // 2
    return pl.pallas_call(
        _attn_kernel,
        grid=(bsz, npair, nct),
        in_specs=[
            pl.BlockSpec((TILE, 2 * HEAD_PAD), lambda b, p, j: (b * nct + j, p)),
            pl.BlockSpec((t, 2 * HEAD_PAD), lambda b, p, j: (b, p)),
            pl.BlockSpec((t, 2 * V_DIM), lambda b, p, j: (b, p)),
        ],
        out_specs=pl.BlockSpec((TILE, 2 * V_DIM), lambda b, p, j: (b * nct + j, p)),
        out_shape=jax.ShapeDtypeStruct((n, HEADS * V_DIM), BF16),
        scratch_shapes=[pltpu.VMEM((ATT_Q, t), F32), pltpu.VMEM((t, 4 * V_DIM), BF16)],
        compiler_params=_cparams(("parallel", "parallel", "arbitrary")),
        name="attn",
    )(q, k, v)


def _scan_chunk(d, s, nct):
    return jnp.where(d == 0, s, jnp.where(s == 0, 0, nct - s))


def _scan_kernel(xc_ref, prev_ref, next_ref, cw_ref, cb_ref, wri_ref, bri_ref, lam_ref,
                 o_ref, xpad, a_s, bx_s, hs_s, carry, *, nct):
    d = pl.program_id(0)
    s = pl.program_id(1)
    chunk = _scan_chunk(d, s, nct)
    bsz = xc_ref.shape[0]
    halo = SUBLANES
    has_prev = jnp.logical_and(chunk != 0, chunk != 1)
    has_next = jnp.logical_and(chunk != 0, chunk != nct - 1)
    xpad[:, halo:halo + TILE, :] = xc_ref[...]
    xpad[:, 0:halo, :] = jnp.where(has_prev, prev_ref[...], 0.0)
    xpad[:, halo + TILE:halo + TILE + halo, :] = jnp.where(has_next, next_ref[...], 0.0)
    xconv = cb_ref[...]
    for kk in range(CONV_W):
        off = halo - CONV_LEFT + kk
        xconv = xconv + xpad[:, off:off + TILE, :] * cw_ref[0, kk:kk + 1, :]
    x2 = xconv.reshape(bsz * TILE, C_WIDTH)
    ri = _dot(x2.astype(BF16), wri_ref[0]) + bri_ref[0]
    r = _sigmoid(ri[:, :C_WIDTH])
    ig = _sigmoid(ri[:, C_WIDTH:])
    lam = lam_ref[0]
    log_sig = jnp.minimum(lam, 0.0) - jnp.log1p(jnp.exp(-jnp.abs(lam)))
    log_a = LRU_C * r * log_sig
    a = jnp.exp(log_a)
    bx = jnp.sqrt(-jnp.tanh(log_a) * (a * a + 1.0)) * (ig * x2)
    nlb = C_WIDTH // LANES
    for cb in range(nlb):
        cols = slice(cb * LANES, (cb + 1) * LANES)
        a_s[cb] = a[:, cols]
        bx_s[cb] = bx[:, cols]

    @pl.when(s == 0)
    def _():
        carry[...] = jnp.zeros_like(carry)

    def step(t, hprev):
        tt = jnp.where(d == 0, t, TILE - 1 - t)
        idx = pl.ds(tt, bsz, stride=TILE)
        hnew = []
        for cb in range(nlb):
            hc = a_s[cb, idx, :] * hprev[cb] + bx_s[cb, idx, :]
            hs_s[cb, idx, :] = hc
            hnew.append(hc)
        return tuple(hnew)

    h0 = tuple(carry[cb] for cb in range(nlb))
    hfin = lax.fori_loop(0, TILE, step, h0, unroll=8)
    for cb in range(nlb):
        carry[cb] = hfin[cb]
        o_ref[0, :, :, cb * LANES:(cb + 1) * LANES] = hs_s[cb].reshape(bsz, TILE, LANES)


def _scan_call(l, xc3, w, nct):
    bsz, t, _ = xc3.shape
    hb = TILE // SUBLANES
    nhb = t // SUBLANES
    cur = lambda d, s: (0, _scan_chunk(d, s, nct), 0)
    prv = lambda d, s: (0, jnp.maximum(_scan_chunk(d, s, nct) * hb - 1, 0), 0)
    nxt = lambda d, s: (0, jnp.minimum((_scan_chunk(d, s, nct) + 1) * hb, nhb - 1), 0)
    return pl.pallas_call(
        functools.partial(_scan_kernel, nct=nct),
        grid=(2, nct),
        in_specs=[
            pl.BlockSpec((bsz, TILE, C_WIDTH), cur),
            pl.BlockSpec((bsz, SUBLANES, C_WIDTH), prv),
            pl.BlockSpec((bsz, SUBLANES, C_WIDTH), nxt),
            pl.BlockSpec((1, CONV_W, C_WIDTH), lambda d, s: (l, 0, 0)),
            pl.BlockSpec((1, 1, C_WIDTH), lambda d, s: (l, 0, 0)),
            pl.BlockSpec((1, C_WIDTH, 2 * C_WIDTH), lambda d, s: (2 * l + d, 0, 0)),
            pl.BlockSpec((1, 1, 2 * C_WIDTH), lambda d, s: (2 * l + d, 0, 0)),
            pl.BlockSpec((1, 1, C_WIDTH), lambda d, s: (2 * l + d, 0, 0)),
        ],
        out_specs=pl.BlockSpec((1, bsz, TILE, C_WIDTH),
                               lambda d, s: (d, 0, _scan_chunk(d, s, nct), 0)),
        out_shape=jax.ShapeDtypeStruct((2, bsz, t, C_WIDTH), F32),
        scratch_shapes=[
            pltpu.VMEM((bsz, TILE + 2 * SUBLANES, C_WIDTH), F32),
            pltpu.VMEM((C_WIDTH // LANES, bsz * TILE, LANES), F32),
            pltpu.VMEM((C_WIDTH // LANES, bsz * TILE, LANES), F32),
            pltpu.VMEM((C_WIDTH // LANES, bsz * TILE, LANES), F32),
            pltpu.VMEM((C_WIDTH // LANES, bsz, LANES), F32),
        ],
        compiler_params=_cparams(("arbitrary", "arbitrary")),
        name="scan",
    )(xc3, xc3, xc3, w["conv_w"], w["conv_b"], w["wri"], w["bri"], w["lam"])


def _top2_sum(v):
    hi1, lo1 = jnp.maximum(v[0], v[1]), jnp.minimum(v[0], v[1])
    hi2, lo2 = jnp.maximum(v[2], v[3]), jnp.minimum(v[2], v[3])
    top = jnp.maximum(hi1, hi2)
    second = jnp.maximum(jnp.minimum(hi1, hi2), jnp.maximum(lo1, lo2))
    return top + second


def _route(scores, sel):
    gs = [_top2_sum(sel[g * EXPERTS_PER_GROUP:(g + 1) * EXPERTS_PER_GROUP]) for g in range(N_GROUPS)]
    best = jnp.zeros_like(gs[0], dtype=jnp.int32)
    best_v = gs[0]
    for g in range(1, N_GROUPS):
        upd = gs[g] > best_v
        best = jnp.where(upd, g, best)
        best_v = jnp.where(upd, gs[g], best_v)

    def pick(rows, k):
        out = rows[k]
        for g in range(1, N_GROUPS):
            out = jnp.where(best == g, rows[g * EXPERTS_PER_GROUP + k], out)
        return out

    in_sel = [pick(sel, k) for k in range(EXPERTS_PER_GROUP)]
    in_sc = [pick(scores, k) for k in range(EXPERTS_PER_GROUP)]
    neg = jnp.full_like(in_sel[0], -jnp.inf)

    def argmax4(vals):
        idx = jnp.zeros_like(best)
        val = vals[0]
        for k in range(1, EXPERTS_PER_GROUP):
            upd = vals[k] > val
            idx = jnp.where(upd, k, idx)
            val = jnp.where(upd, vals[k], val)
        return idx

    i1 = argmax4(in_sel)
    i2 = argmax4([jnp.where(i1 == k, neg, in_sel[k]) for k in range(EXPERTS_PER_GROUP)])

    def take(vals, idx):
        out = vals[0]
        for k in range(1, EXPERTS_PER_GROUP):
            out = jnp.where(idx == k, vals[k], out)
        return out

    w1 = take(in_sc, i1)
    w2 = take(in_sc, i2)
    den = w1 + w2
    return (best * EXPERTS_PER_GROUP + i1, best * EXPERTS_PER_GROUP + i2,
            ROUTED_SCALE * w1 / den, ROUTED_SCALE * w2 / den)


def _merge_kernel(x_ref, h_ref, ya_ref, yb_ref, cg_ref, rf_ref, rb_ref, mod_ref,
                  wg_ref, wbr_ref, wo_ref, lng_ref, lnb_ref, wrt_ref, rbias_ref,
                  x1_ref, f_ref, re_ref, rw_ref, rr_ref, *, alpha):
    h = h_ref[...]
    yc = (cg_ref[...] * (rf_ref[0] + rb_ref[0])).astype(BF16)
    ys = (ya_ref[...], yb_ref[...], yc)
    mix = None
    for kb in range(N_BRANCH):
        gate = _sigmoid(_dot(h, wg_ref[0, :, kb * D:(kb + 1) * D]))
        term = gate * _dot(ys[kb], wbr_ref[0, kb])
        mix = term if mix is None else mix + term
    y = _dot(mix.astype(BF16), wo_ref[0])

    row_i = lax.broadcasted_iota(jnp.int32, (TILE, TILE), 0)
    col_i = lax.broadcasted_iota(jnp.int32, (TILE, TILE), 1)
    tri = jnp.where(row_i <= col_i, 1.0, 0.0).astype(BF16)
    for sub in range(MM_TILE // TILE):
        rows = slice(sub * TILE, (sub + 1) * TILE)
        m = mod_ref[sub]
        g1 = m[:, 2 * D:3 * D]
        sh2 = m[:, 3 * D:4 * D]
        sc2 = m[:, 4 * D:5 * D]
        x1 = _layer_norm(alpha * x_ref[rows, :] + g1 * y[rows, :], lng_ref[0], lnb_ref[0])
        x1_ref[rows, :] = x1
        f = x1 * (1.0 + sc2) + sh2
        f_ref[rows, :] = f

        logits = lax.dot_general(wrt_ref[...], f, (((1,), (1,)), ((), ())),
                                 precision=lax.Precision.HIGHEST, preferred_element_type=F32)
        sc = jax.nn.sigmoid(logits)
        sl = sc + rbias_ref[...]
        scores = [sc[e:e + 1, :] for e in range(N_EXPERTS)]
        sel = [sl[e:e + 1, :] for e in range(N_EXPERTS)]
        e1, e2, w1, w2 = _route(scores, sel)
        re_ref[:, rows] = jnp.concatenate([e1, e2], axis=0)
        rw_ref[:, rows] = jnp.concatenate([w1, w2], axis=0)
        ranks = []
        for ek in (e1, e2):
            onehot = jnp.concatenate([jnp.where(ek == e, 1.0, 0.0) for e in range(N_EXPERTS)], axis=0)
            prefix = _dot(onehot.astype(BF16), tri)
            ranks.append(jnp.sum(onehot * prefix, axis=0, keepdims=True) - 1.0)
        rr_ref[:, rows] = jnp.concatenate(ranks, axis=0).astype(jnp.int32)


def _merge_call(l, x, h, ya, yb, cg, rec, mod_tiles, w, wrt, rbias, alpha):
    n = x.shape[0]
    tok = lambda width: pl.BlockSpec((MM_TILE, width), lambda i: (i, 0))
    names = ["wg", "w_br", "w_o", "ln1_g", "ln1_b"]
    return pl.pallas_call(
        functools.partial(_merge_kernel, alpha=alpha),
        grid=(n // MM_TILE,),
        in_specs=[
            tok(D), tok(D), tok(A_WIDTH), tok(HEADS * V_DIM), tok(C_WIDTH),
            pl.BlockSpec((1, MM_TILE, C_WIDTH), lambda i: (0, i, 0)),
            pl.BlockSpec((1, MM_TILE, C_WIDTH), lambda i: (1, i, 0)),
            pl.BlockSpec((MM_TILE // TILE, 1, 6 * D), lambda i: (i, 0, 0)),
        ] + [_const_spec(w[nm], l) for nm in names] + [
            pl.BlockSpec(wrt.shape, lambda i: (0, 0)),
            pl.BlockSpec(rbias.shape, lambda i: (0, 0)),
        ],
        out_specs=[tok(D), tok(D)] + [pl.BlockSpec((2, MM_TILE), lambda i: (0, i))] * 3,
        out_shape=[jax.ShapeDtypeStruct((n, D), F32), jax.ShapeDtypeStruct((n, D), F32),
                   jax.ShapeDtypeStruct((2, n), jnp.int32), jax.ShapeDtypeStruct((2, n), F32),
                   jax.ShapeDtypeStruct((2, n), jnp.int32)],
        compiler_params=_cparams(("parallel",)),
        name="merge",
    )(x, h, ya, yb, cg, rec, rec, mod_tiles, w["wg"], w["w_br"], w["w_o"], w["ln1_g"], w["ln1_b"],
      wrt, rbias)


def _row_gather_copy(src_hbm, row, dst, dst_row, sem):
    return pltpu.make_async_copy(src_hbm.at[pl.ds(row, 1), :], dst.at[pl.ds(dst_row, 1), :], sem)


def _tile_wait(src_hbm, dst, sem):
    pltpu.make_async_copy(src_hbm.at[pl.ds(0, dst.shape[0]), :], dst, sem).wait()


def _expert_kernel(src_ref, texp_ref, nvalid_ref, f_hbm, wgt_ref, wup_ref, wdn_ref,
                   o_ref, buf, sem):
    i = pl.program_id(0)
    nvalid = nvalid_ref[0]
    slot = i % 2

    def issue(tile, sl):
        base = tile * MOE_TILE
        for r in range(MOE_TILE):
            _row_gather_copy(f_hbm, src_ref[base + r], buf.at[sl], r, sem.at[sl]).start()

    @pl.when(jnp.logical_and(i == 0, nvalid > 0))
    def _():
        issue(0, 0)

    @pl.when(i + 1 < nvalid)
    def _():
        issue(i + 1, 1 - slot)

    @pl.when(i < nvalid)
    def _():
        _tile_wait(f_hbm, buf.at[slot], sem.at[slot])
        xin = buf[slot].astype(BF16)
        gt = _dot(xin, wgt_ref[0, 0])
        t = (gt * _sigmoid(gt) * _dot(xin, wup_ref[0, 0])).astype(BF16)
        o_ref[...] = _dot(t, wdn_ref[0, 0])

    @pl.when(i >= nvalid)
    def _():
        o_ref[...] = jnp.zeros_like(o_ref)


def _expert_call(l, f, src_tok, tile_expert, nvalid, w):
    ntile = tile_expert.shape[0]
    wspec = lambda a, b: pl.BlockSpec((1, 1, a, b), lambda i, src, texp, nv: (l, texp[i], 0, 0))
    return pl.pallas_call(
        _expert_kernel,
        grid_spec=pltpu.PrefetchScalarGridSpec(
            num_scalar_prefetch=3,
            grid=(ntile,),
            in_specs=[pl.BlockSpec(memory_space=pl.ANY),
                      wspec(D, D_EXPERT), wspec(D, D_EXPERT), wspec(D_EXPERT, D)],
            out_specs=pl.BlockSpec((MOE_TILE, D), lambda i, src, texp, nv: (i, 0)),
            scratch_shapes=[pltpu.VMEM((2, MOE_TILE, D), F32), pltpu.SemaphoreType.DMA((2,))],
        ),
        out_shape=jax.ShapeDtypeStruct((ntile * MOE_TILE, D), F32),
        compiler_params=_cparams(("arbitrary",)),
        name="expert",
    )(src_tok, tile_expert, nvalid, f, w["w_gate"], w["w_up"], w["w_down"])


def _combine_kernel(pos_ref, ys_hbm, rw_ref, x1_ref, mod_ref, lng_ref, lnb_ref, o_ref, buf, sem,
                    *, alpha, n):
    i = pl.program_id(0)
    slot = i % 2

    def issue(tile, sl):
        for k in range(2):
            base = k * n + tile * TILE
            for r in range(TILE):
                _row_gather_copy(ys_hbm, pos_ref[base + r], buf.at[sl, k], r, sem.at[sl]).start()

    @pl.when(i == 0)
    def _():
        issue(0, 0)

    @pl.when(i + 1 < pl.num_programs(0))
    def _():
        issue(i + 1, 1 - slot)

    for k in range(2):
        _tile_wait(ys_hbm, buf.at[slot, k], sem.at[slot])
    rw = rw_ref[...]
    fo = rw[:, 0:1] * buf[slot, 0] + rw[:, 1:2] * buf[slot, 1]
    g2 = mod_ref[0, :, 5 * D:6 * D]
    o_ref[...] = _layer_norm(alpha * x1_ref[...] + g2 * fo, lng_ref[0], lnb_ref[0])


def _combine_call(l, pos, ys, rw_t, x1, mod_tiles, w, alpha):
    n = x1.shape[0]
    nt = n // TILE
    tok = lambda width: pl.BlockSpec((TILE, width), lambda i, p: (i, 0))
    lay3 = lambda a: pl.BlockSpec((1,) + a.shape[1:], lambda i, p: (l, 0, 0))
    return pl.pallas_call(
        functools.partial(_combine_kernel, alpha=alpha, n=n),
        grid_spec=pltpu.PrefetchScalarGridSpec(
            num_scalar_prefetch=1,
            grid=(nt,),
            in_specs=[pl.BlockSpec(memory_space=pl.ANY), tok(2), tok(D),
                      pl.BlockSpec((1, 1, 6 * D), lambda i, p: (i, 0, 0)),
                      lay3(w["ln2_g"]), lay3(w["ln2_b"])],
            out_specs=tok(D),
            scratch_shapes=[pltpu.VMEM((2, 2, TILE, D), F32), pltpu.SemaphoreType.DMA((2,))],
        ),
        out_shape=jax.ShapeDtypeStruct((n, D), F32),
        compiler_params=_cparams(("arbitrary",)),
        name="combine",
    )(pos, ys, rw_t, x1, mod_tiles, w["ln2_g"], w["ln2_b"])


def _invert_kernel(pos_ref, pad_lo_ref, pad_hi_ref, src_ref):
    ntok = pos_ref.shape[0] // 2
    for e in range(pad_lo_ref.shape[0]):
        def clear(i, c):
            src_ref[i] = 0
            return c

        lax.fori_loop(pad_lo_ref[e], pad_hi_ref[e], clear, 0)
    for k in range(2):
        def place(t, c, k=k):
            src_ref[pos_ref[k * ntok + t]] = t
            return c

        lax.fori_loop(0, ntok, place, 0, unroll=16)


def _invert_call(pos, pad_lo, pad_hi, nslots):
    smem = pl.BlockSpec(memory_space=pltpu.SMEM)
    return pl.pallas_call(
        _invert_kernel,
        in_specs=[smem, smem, smem],
        out_specs=smem,
        out_shape=jax.ShapeDtypeStruct((nslots,), jnp.int32),
        name="invert",
    )(pos, pad_lo, pad_hi)


def _dispatch_plan(re, rr, nt):
    n = re.shape[1]
    onehot = re.reshape(2, nt, TILE, 1) == jnp.arange(N_EXPERTS, dtype=jnp.int32)
    counts = jnp.sum(onehot, axis=2, dtype=jnp.int32)
    flat = jnp.swapaxes(counts, 0, 1).reshape(nt * 2, N_EXPERTS)
    incl = jnp.cumsum(flat, axis=0)
    total = incl[-1]
    off = jnp.swapaxes((incl - flat).reshape(nt, 2, N_EXPERTS), 0, 1)
    padded = (total + MOE_TILE - 1) // MOE_TILE * MOE_TILE
    ends = jnp.cumsum(padded)
    base = ends - padded
    start = (base + off)[:, :, None, :]
    pos = jnp.sum(jnp.where(onehot, start, 0), axis=-1).reshape(2, n) + rr
    ntile = (2 * n) // MOE_TILE + N_EXPERTS
    tile_start = jnp.arange(ntile, dtype=jnp.int32) * MOE_TILE
    tile_expert = jnp.minimum(jnp.sum(ends[None, :] <= tile_start[:, None], axis=1), N_EXPERTS - 1)
    nvalid = (ends[-1] // MOE_TILE).reshape(1)
    pos = pos.reshape(-1).astype(jnp.int32)
    nslots = ntile * MOE_TILE
    pad_lo = jnp.concatenate([base + total, ends[-1:]]).astype(jnp.int32)
    pad_hi = jnp.concatenate([ends, jnp.full((1,), nslots, ends.dtype)]).astype(jnp.int32)
    src_tok = _invert_call(pos, pad_lo, pad_hi, nslots)
    return pos, src_tok, tile_expert.astype(jnp.int32), nvalid.astype(jnp.int32)


def _rope_tables(seq, ctx_len):
    rows = seq // GRID_W
    row = jnp.repeat(jnp.arange(rows, dtype=F32), GRID_W)
    col = jnp.tile(jnp.arange(GRID_W, dtype=F32), rows)
    n_freq = QK_ROPE // 4
    inv = ROPE_BASE ** (-jnp.arange(n_freq, dtype=F32) / n_freq)
    ang = jnp.stack([row[:, None] * inv, col[:, None] * inv], axis=1)
    cos, sin = jnp.cos(ang), jnp.sin(ang)
    cos32 = jnp.broadcast_to(cos[:, :, None, :], (seq, 2, 2, n_freq)).reshape(seq, QK_ROPE)
    sign = jnp.array([-1.0, 1.0], F32)[None, None, :, None]
    sin32 = (jnp.broadcast_to(sin[:, :, None, :], (seq, 2, 2, n_freq)) * sign).reshape(seq, QK_ROPE)
    t = ctx_len + seq
    cos_t = jnp.ones((t, HEAD_PAD), F32).at[ctx_len:, ROPE_LANE0:ROPE_LANE0 + QK_ROPE].set(cos32)
    sin_t = jnp.zeros((t, HEAD_PAD), F32).at[ctx_len:, ROPE_LANE0:ROPE_LANE0 + QK_ROPE].set(sin32)
    return cos_t, sin_t


def _swap_perm():
    n_freq = QK_ROPE // 4
    r = np.arange(QK_ROPE)
    axis, half, f = r // (2 * n_freq), (r // n_freq) % 2, r % n_freq
    return axis * 2 * n_freq + (1 - half) * n_freq + f


def _pack_weights(w_in, w_uq, w_ukv, w_s, b_s, w_r, b_r, w_i, b_i, lru_lambda):
    depth = w_in.shape[0]
    perm = _swap_perm()
    dq = QK_NOPE + QK_ROPE
    wq = w_uq.reshape(depth, Q_RANK, HEADS, dq)
    zq = jnp.zeros((depth, Q_RANK, HEADS, HEAD_PAD - dq), w_uq.dtype)
    main = jnp.concatenate([wq, zq], axis=-1)
    swp = jnp.concatenate([jnp.zeros_like(wq[..., :QK_NOPE]), wq[..., QK_NOPE + perm], zq], axis=-1)
    wuq = jnp.concatenate([main.reshape(depth, Q_RANK, -1), swp.reshape(depth, Q_RANK, -1)], axis=-1)
    wkv = w_ukv.reshape(depth, KV_RANK, HEADS, QK_NOPE + V_DIM)
    wuk = jnp.concatenate([wkv[..., :QK_NOPE], jnp.zeros((depth, KV_RANK, HEADS, HEAD_PAD - QK_NOPE), w_ukv.dtype)],
                          axis=-1).reshape(depth, KV_RANK, -1)
    wuv = wkv[..., QK_NOPE:].reshape(depth, KV_RANK, -1)
    wkr = w_in[:, :, OFF_KR:OFF_KR + QK_ROPE]
    zl = jnp.zeros((depth, D, ROPE_LANE0), w_in.dtype)
    zr = jnp.zeros((depth, D, HEAD_PAD - ROPE_LANE0 - QK_ROPE), w_in.dtype)
    wb = jnp.concatenate([w_in[:, :, OFF_CQ:OFF_KR], zl, wkr, zr, zl, wkr[:, :, perm], zr], axis=-1)
    eye = jnp.eye(C_BLOCKS, dtype=w_r.dtype)

    def dense(wblk):
        full = jnp.einsum("ldhij,hk->ldhikj", wblk, eye)
        return full.reshape(depth, 2, C_WIDTH, C_WIDTH)

    wri = jnp.concatenate([dense(w_r), dense(w_i)], axis=-1).reshape(depth * 2, C_WIDTH, 2 * C_WIDTH)
    bri = jnp.concatenate([b_r, b_i], axis=-1).reshape(depth * 2, 1, 2 * C_WIDTH)
    gw = A_WIDTH // A_GROUPS
    bs_full = jnp.broadcast_to(jnp.swapaxes(b_s, 1, 2)[:, :, :, None],
                               (depth, CHUNK, A_GROUPS, gw)).reshape(depth, CHUNK, A_WIDTH)
    return {
        "wa": w_in[:, :, OFF_AU:OFF_CQ].astype(BF16),
        "wb": wb.astype(BF16),
        "wc": w_in[:, :, OFF_CX:OFF_GATE].astype(BF16),
        "wg": w_in[:, :, OFF_GATE:].astype(BF16),
        "wuq": wuq.astype(BF16), "wuk": wuk.astype(BF16), "wuv": wuv.astype(BF16),
        "w_s": w_s.astype(BF16), "bs_full": bs_full,
        "wri": wri.astype(BF16), "bri": bri,
        "lam": lru_lambda.reshape(depth * 2, 1, C_WIDTH),
    }


def kernel(x, c, ctx, c_ctx, w_ada, b_ada, w_in, a_ln_g, a_ln_b, w_s, b_s, q_norm_g, w_uq,
           kv_norm_g, w_ukv, conv_w, conv_b, w_r, b_r, w_i, b_i, lru_lambda, w_br, w_o,
           ln1_g, ln1_b, w_router, router_bias, w_gate, w_up, w_down, ln2_g, ln2_b):
    depth = w_in.shape[0]
    bsz, seq, _ = x.shape
    ctx_len = ctx.shape[1]
    assert ctx_len == TILE and seq % TILE == 0 and bsz == SUBLANES
    t = ctx_len + seq
    nct = t // TILE
    n = bsz * t
    assert n % MM_TILE == 0 and (2 * n) % MOE_TILE == 0
    alpha = (2.0 * depth) ** 0.25

    row3 = lambda a: a.reshape(depth, 1, a.shape[-1])
    w = _pack_weights(w_in, w_uq, w_ukv, w_s, b_s, w_r, b_r, w_i, b_i, lru_lambda)
    w.update({
        "a_ln_g": row3(a_ln_g), "a_ln_b": row3(a_ln_b),
        "q_norm_g": row3(q_norm_g), "kv_norm_g": row3(kv_norm_g),
        "conv_w": conv_w, "conv_b": row3(conv_b),
        "w_br": w_br.astype(BF16), "w_o": w_o.astype(BF16),
        "ln1_g": row3(ln1_g), "ln1_b": row3(ln1_b), "ln2_g": row3(ln2_g), "ln2_b": row3(ln2_b),
        "w_gate": w_gate.astype(BF16), "w_up": w_up.astype(BF16), "w_down": w_down.astype(BF16),
    })
    cos_t, sin_t = _rope_tables(seq, ctx_len)
    wrt = w_router.T
    rbias = router_bias.reshape(N_EXPERTS, 1)

    cond_rows = jnp.zeros((2 * SUBLANES, D), F32).at[:bsz].set(c).at[bsz].set(c_ctx)
    mod = _mod_call(cond_rows, w_ada, b_ada)
    tile_row = np.array([bsz if j == 0 else b for b in range(bsz) for j in range(nct)], np.int32)
    mod_tiles = mod[:, tile_row, :].reshape(depth, n // TILE, 1, 6 * D)

    xs = jnp.concatenate([ctx, x], axis=1).reshape(n, D)
    for l in range(depth):
        mt = mod_tiles[l]
        h, ya, q, k, v, xc, cg = _proj_call(l, xs, mt, cos_t, sin_t, w, nct)
        yb = _attn_call(q, k, v, bsz, nct)
        rec = _scan_call(l, xc.reshape(bsz, t, C_WIDTH), w, nct).reshape(2, n, C_WIDTH)
        x1, f, re, rw, rr = _merge_call(l, xs, h, ya, yb, cg, rec, mt, w, wrt, rbias, alpha)
        pos, src_tok, tile_expert, nvalid = _dispatch_plan(re, rr, n // TILE)
        ys = _expert_call(l, f, src_tok, tile_expert, nvalid, w)
        xs = _combine_call(l, pos, ys, rw.T, x1, mt, w, alpha)
    return xs.reshape(bsz, t, D)[:, ctx_len:, :]
```

```python
import functools

import numpy as np
import jax
import jax.numpy as jnp
from jax import lax
from jax.experimental import pallas as pl
from jax.experimental.pallas import tpu as pltpu

D = 1024
GRID_W = 64
A_WIDTH = 512
A_GROUPS = 4
CHUNK = 128
HEADS = 8
QK_NOPE = 64
QK_ROPE = 32
V_DIM = 64
Q_RANK = 256
KV_RANK = 128
ROPE_BASE = 10000.0
C_WIDTH = 512
C_BLOCKS = 8
C_BLOCK_W = C_WIDTH // C_BLOCKS
CONV_W = 4
CONV_LEFT = 2
LRU_C = 8.0
N_BRANCH = 3
BRANCH_W = 512
OFF_AU = 0
OFF_AV = OFF_AU + A_WIDTH
OFF_CQ = OFF_AV + A_WIDTH
OFF_CKV = OFF_CQ + Q_RANK
OFF_KR = OFF_CKV + KV_RANK
OFF_CX = OFF_KR + QK_ROPE
OFF_CG = OFF_CX + C_WIDTH
OFF_GATE = OFF_CG + C_WIDTH
N_EXPERTS = 16
N_GROUPS = 4
EXPERTS_PER_GROUP = N_EXPERTS // N_GROUPS
D_EXPERT = 512
ROUTED_SCALE = 2.5
EPS = 1e-6

LANES = 128
SUBLANES = 8
TILE = 256
HEAD_PAD = 128
ROPE_LANE0 = QK_NOPE
MOE_TILE = 256
ATT_Q = 128
ATT_K = 2 * LANES
MM_TILE = 512
VMEM_LIMIT = 56 * 1024 * 1024

BF16 = jnp.bfloat16
F32 = jnp.float32


def _cparams(sem):
    return pltpu.CompilerParams(dimension_semantics=sem, vmem_limit_bytes=VMEM_LIMIT)


def _dot(a, b):
    return jnp.dot(a, b, preferred_element_type=F32)


def _sigmoid(v):
    return 0.5 * jnp.tanh(0.5 * v) + 0.5


def _layer_norm(v, g, b):
    mu = jnp.mean(v, axis=-1, keepdims=True)
    c = v - mu
    var = jnp.mean(c * c, axis=-1, keepdims=True)
    return (c * lax.rsqrt(var + EPS)) * g + b


def _rms_norm(v, g):
    ms = jnp.mean(v * v, axis=-1, keepdims=True)
    return (v * lax.rsqrt(ms + EPS)) * g


def _mod_kernel(c_ref, w_ref, b_ref, o_ref):
    c = c_ref[...]
    cond = (c * jax.nn.sigmoid(c)).astype(BF16)
    o_ref[0] = _dot(cond, w_ref[0].astype(BF16)) + b_ref[0]


def _mod_call(cond_rows, w_ada, b_ada):
    depth = w_ada.shape[0]
    rows = cond_rows.shape[0]
    ncol = 6 * D
    bn = 1536
    return pl.pallas_call(
        _mod_kernel,
        grid=(depth, ncol // bn),
        in_specs=[
            pl.BlockSpec((rows, D), lambda l, j: (0, 0)),
            pl.BlockSpec((1, D, bn), lambda l, j: (l, 0, j)),
            pl.BlockSpec((1, 1, bn), lambda l, j: (l, 0, j)),
        ],
        out_specs=pl.BlockSpec((1, rows, bn), lambda l, j: (l, 0, j)),
        out_shape=jax.ShapeDtypeStruct((depth, rows, ncol), F32),
        compiler_params=_cparams(("parallel", "parallel")),
        name="mod",
    )(cond_rows, w_ada, b_ada.reshape(depth, 1, ncol))


def _proj_kernel(x_ref, mod_ref, cos0_ref, cos1_ref, sin0_ref, sin1_ref, wa_ref, wb_ref, wc_ref,
                 alng_ref, alnb_ref, ws_ref, bs_ref, qg_ref, wuq_ref, kvg_ref,
                 wuk_ref, wuv_ref,
                 h_ref, ya_ref, q_ref, k_ref, v_ref, xc_ref, cg_ref):
    halves = []
    for sub in range(MM_TILE // TILE):
        rows = slice(sub * TILE, (sub + 1) * TILE)
        m = mod_ref[sub]
        halves.append((x_ref[rows, :] * (1.0 + m[:, D:2 * D]) + m[:, 0:D]).astype(BF16))
    h = jnp.concatenate(halves, axis=0)
    h_ref[...] = h

    uv = _dot(h, wa_ref[0])
    u = jax.nn.gelu(uv[:, :A_WIDTH])
    v = _layer_norm(jax.nn.gelu(uv[:, A_WIDTH:]), alng_ref[0], alnb_ref[0]).astype(BF16)
    gw = A_WIDTH // A_GROUPS
    for c in range(MM_TILE // CHUNK):
        rows = slice(c * CHUNK, (c + 1) * CHUNK)
        for g in range(A_GROUPS):
            cols = slice(g * gw, (g + 1) * gw)
            mixed = _dot(ws_ref[0, g], v[rows, cols]) + bs_ref[0, :, cols]
            ya_ref[rows, cols] = (u[rows, cols] * mixed).astype(BF16)

    pb = _dot(h, wb_ref[0])
    cos_t = jnp.concatenate([cos0_ref[...], cos1_ref[...]], axis=0)
    sin_t = jnp.concatenate([sin0_ref[...], sin1_ref[...]], axis=0)
    cqn = _rms_norm(pb[:, :Q_RANK], qg_ref[0]).astype(BF16)
    qq = _dot(cqn, wuq_ref[0])
    hw = HEADS * HEAD_PAD
    for hd in range(HEADS):
        a = slice(hd * HEAD_PAD, (hd + 1) * HEAD_PAD)
        b = slice(hw + hd * HEAD_PAD, hw + (hd + 1) * HEAD_PAD)
        q_ref[:, a] = (qq[:, a] * cos_t + qq[:, b] * sin_t).astype(BF16)
    ckvn = _rms_norm(pb[:, Q_RANK:Q_RANK + KV_RANK], kvg_ref[0]).astype(BF16)
    o_kr = Q_RANK + KV_RANK
    kr = pb[:, o_kr:o_kr + HEAD_PAD] * cos_t + pb[:, o_kr + HEAD_PAD:o_kr + 2 * HEAD_PAD] * sin_t
    kk = _dot(ckvn, wuk_ref[0])
    for hd in range(HEADS):
        a = slice(hd * HEAD_PAD, (hd + 1) * HEAD_PAD)
        k_ref[:, a] = (kk[:, a] + kr).astype(BF16)
    v_ref[...] = _dot(ckvn, wuv_ref[0]).astype(BF16)

    pc = _dot(h, wc_ref[0])
    xc_ref[...] = pc[:, :C_WIDTH]
    cg_ref[...] = jax.nn.gelu(pc[:, C_WIDTH:])


def _const_spec(a, l):
    nz = (0,) * (a.ndim - 1)
    return pl.BlockSpec((1,) + a.shape[1:], lambda i: (l,) + nz, pipeline_mode=pl.Buffered(1))


def _proj_call(l, x, mod_tiles, cos_t, sin_t, w, nct):
    n = x.shape[0]
    nsub = MM_TILE // TILE
    assert nsub == 2
    tok = lambda width: pl.BlockSpec((MM_TILE, width), lambda i: (i, 0))
    pos0 = pl.BlockSpec((TILE, HEAD_PAD), lambda i: ((nsub * i) % nct, 0))
    pos1 = pl.BlockSpec((TILE, HEAD_PAD), lambda i: ((nsub * i + 1) % nct, 0))
    outs = [
        (D, BF16), (A_WIDTH, BF16), (HEADS * HEAD_PAD, BF16), (HEADS * HEAD_PAD, BF16),
        (HEADS * V_DIM, BF16), (C_WIDTH, F32), (C_WIDTH, F32),
    ]
    names = ["wa", "wb", "wc", "a_ln_g", "a_ln_b", "w_s", "bs_full", "q_norm_g", "wuq",
             "kv_norm_g", "wuk", "wuv"]
    return pl.pallas_call(
        _proj_kernel,
        grid=(n // MM_TILE,),
        in_specs=[
            tok(D),
            pl.BlockSpec((nsub, 1, 6 * D), lambda i: (i, 0, 0)),
            pos0, pos1, pos0, pos1,
        ] + [_const_spec(w[nm], l) for nm in names],
        out_specs=[tok(wd) for wd, _ in outs],
        out_shape=[jax.ShapeDtypeStruct((n, wd), dt) for wd, dt in outs],
        compiler_params=_cparams(("parallel",)),
        name="proj",
    )(x, mod_tiles, cos_t, cos_t, sin_t, sin_t, *[w[nm] for nm in names])


def _attn_kernel(q_ref, k_ref, v_ref, o_ref, s_buf, vext):
    j = pl.program_id(2)
    c2 = float(QK_NOPE + QK_ROPE) ** -0.5 * float(np.log2(np.e))
    nt_dims = (((1,), (1,)), ((), ()))
    nv = 2 * V_DIM

    @pl.when(j == 0)
    def _():
        vext[:, :nv] = v_ref[...]
        vext[:, nv:] = jnp.ones((vext.shape[0], vext.shape[1] - nv), BF16)

    def attend(nk):
        for qh in range(TILE // ATT_Q):
            rows = slice(qh * ATT_Q, (qh + 1) * ATT_Q)
            for hd in range(2):
                cols = slice(hd * HEAD_PAD, (hd + 1) * HEAD_PAD)
                q = q_ref[rows, cols]
                m = None
                for c in range(nk // ATT_K):
                    keys = slice(c * ATT_K, (c + 1) * ATT_K)
                    s = lax.dot_general(q, k_ref[keys, cols], nt_dims, preferred_element_type=F32)
                    s_buf[:, keys] = s
                    sm = jnp.maximum(s[:, :LANES], s[:, LANES:])
                    m = sm if m is None else jnp.maximum(m, sm)
                mx = jnp.max(m, axis=-1, keepdims=True)
                acc = None
                for c in range(nk // ATT_K):
                    keys = slice(c * ATT_K, (c + 1) * ATT_K)
                    p = jnp.exp2((s_buf[:, keys] - mx) * c2)
                    pv = _dot(p.astype(BF16), vext[keys, :])
                    acc = pv if acc is None else acc + pv
                out = (acc[:, :nv] / acc[:, nv:]).astype(BF16)
                vcols = slice(hd * V_DIM, (hd + 1) * V_DIM)
                o_ref[rows, vcols] = out[:, vcols]

    @pl.when(j == 0)
    def _():
        attend(TILE)

    @pl.when(j > 0)
    def _():
        attend(k_ref.shape[0])


def _attn_call(q, k, v, bsz, nct):
    n = q.shape[0]
    t = nct * TILE
    npair = HEADS // 2
    return pl.pallas_call(
        _attn_kernel,
        grid=(bsz, npair, nct),
        in_specs=[
            pl.BlockSpec((TILE, 2 * HEAD_PAD), lambda b, p, j: (b * nct + j, p)),
            pl.BlockSpec((t, 2 * HEAD_PAD), lambda b, p, j: (b, p)),
            pl.BlockSpec((t, 2 * V_DIM), lambda b, p, j: (b, p)),
        ],
        out_specs=pl.BlockSpec((TILE, 2 * V_DIM), lambda b, p, j: (b * nct + j, p)),
        out_shape=jax.ShapeDtypeStruct((n, HEADS * V_DIM), BF16),
        scratch_shapes=[pltpu.VMEM((ATT_Q, t), F32), pltpu.VMEM((t, 4 * V_DIM), BF16)],
        compiler_params=_cparams(("parallel", "parallel", "arbitrary")),
        name="attn",
    )(q, k, v)


def _scan_chunk(d, s, nct):
    return jnp.where(d == 0, s, jnp.where(s == 0, 0, nct - s))


def _scan_kernel(xc_ref, prev_ref, next_ref, cw_ref, cb_ref, wri_ref, bri_ref, lam_ref,
                 o_ref, a_s, bx_s, carry, *, nct):
    d = pl.program_id(0)
    s = pl.program_id(1)
    chunk = _scan_chunk(d, s, nct)
    bsz = xc_ref.shape[1]
    has_prev = jnp.logical_and(chunk != 0, chunk != 1)
    has_next = jnp.logical_and(chunk != 0, chunk != nct - 1)
    cur = xc_ref[...]
    left = CONV_LEFT
    right = CONV_W - 1 - CONV_LEFT
    xp = jnp.concatenate([jnp.where(has_prev, prev_ref[...], 0.0), cur,
                          jnp.where(has_next, next_ref[...], 0.0)], axis=0)
    xconv = cb_ref[...]
    for kk in range(CONV_W):
        xconv = xconv + xp[kk:kk + TILE] * cw_ref[0, kk:kk + 1, :]
    assert xp.shape[0] == TILE + left + right
    x2 = xconv.reshape(TILE * bsz, C_WIDTH)
    ri = _dot(x2.astype(BF16), wri_ref[0]) + bri_ref[0]
    r = _sigmoid(ri[:, :C_WIDTH])
    ig = _sigmoid(ri[:, C_WIDTH:])
    lam = lam_ref[0]
    log_sig = jnp.minimum(lam, 0.0) - jnp.log1p(jnp.exp(-jnp.abs(lam)))
    log_a = LRU_C * r * log_sig
    a = jnp.exp(log_a)
    a_s[...] = a
    bx_s[...] = jnp.sqrt(-jnp.tanh(log_a) * (a * a + 1.0)) * (ig * x2)

    @pl.when(s == 0)
    def _():
        carry[...] = jnp.zeros_like(carry)

    def step(t, hprev):
        tt = jnp.where(d == 0, t, TILE - 1 - t)
        rows = pl.ds(pl.multiple_of(tt * bsz, bsz), bsz)
        hnew = a_s[rows, :] * hprev + bx_s[rows, :]
        o_ref[0, tt] = hnew
        return hnew

    carry[...] = lax.fori_loop(0, TILE, step, carry[...], unroll=8)


def _scan_call(l, xc3, w, nct):
    t, bsz, _ = xc3.shape
    left = CONV_LEFT
    right = CONV_W - 1 - CONV_LEFT
    assert TILE % left == 0 and right == 1
    cur = lambda d, s: (_scan_chunk(d, s, nct), 0, 0)
    prv = lambda d, s: (jnp.maximum(_scan_chunk(d, s, nct) * (TILE // left) - 1, 0), 0, 0)
    nxt = lambda d, s: (jnp.minimum((_scan_chunk(d, s, nct) + 1) * TILE, t - 1), 0, 0)
    return pl.pallas_call(
        functools.partial(_scan_kernel, nct=nct),
        grid=(2, nct),
        in_specs=[
            pl.BlockSpec((TILE, bsz, C_WIDTH), cur),
            pl.BlockSpec((left, bsz, C_WIDTH), prv),
            pl.BlockSpec((right, bsz, C_WIDTH), nxt),
            pl.BlockSpec((1, CONV_W, C_WIDTH), lambda d, s: (l, 0, 0)),
            pl.BlockSpec((1, 1, C_WIDTH), lambda d, s: (l, 0, 0)),
            pl.BlockSpec((1, C_WIDTH, 2 * C_WIDTH), lambda d, s: (2 * l + d, 0, 0)),
            pl.BlockSpec((1, 1, 2 * C_WIDTH), lambda d, s: (2 * l + d, 0, 0)),
            pl.BlockSpec((1, 1, C_WIDTH), lambda d, s: (2 * l + d, 0, 0)),
        ],
        out_specs=pl.BlockSpec((1, TILE, bsz, C_WIDTH),
                               lambda d, s: (d, _scan_chunk(d, s, nct), 0, 0)),
        out_shape=jax.ShapeDtypeStruct((2, t, bsz, C_WIDTH), F32),
        scratch_shapes=[
            pltpu.VMEM((TILE * bsz, C_WIDTH), F32),
            pltpu.VMEM((TILE * bsz, C_WIDTH), F32),
            pltpu.VMEM((bsz, C_WIDTH), F32),
        ],
        compiler_params=_cparams(("arbitrary", "arbitrary")),
        name="scan",
    )(xc3, xc3, xc3, w["conv_w"], w["conv_b"], w["wri"], w["bri"], w["lam"])


def _top2_sum(v):
    hi1, lo1 = jnp.maximum(v[0], v[1]), jnp.minimum(v[0], v[1])
    hi2, lo2 = jnp.maximum(v[2], v[3]), jnp.minimum(v[2], v[3])
    top = jnp.maximum(hi1, hi2)
    second = jnp.maximum(jnp.minimum(hi1, hi2), jnp.maximum(lo1, lo2))
    return top + second


def _route(scores, sel):
    gs = [_top2_sum(sel[g * EXPERTS_PER_GROUP:(g + 1) * EXPERTS_PER_GROUP]) for g in range(N_GROUPS)]
    best = jnp.zeros_like(gs[0], dtype=jnp.int32)
    best_v = gs[0]
    for g in range(1, N_GROUPS):
        upd = gs[g] > best_v
        best = jnp.where(upd, g, best)
        best_v = jnp.where(upd, gs[g], best_v)

    def pick(rows, k):
        out = rows[k]
        for g in range(1, N_GROUPS):
            out = jnp.where(best == g, rows[g * EXPERTS_PER_GROUP + k], out)
        return out

    in_sel = [pick(sel, k) for k in range(EXPERTS_PER_GROUP)]
    in_sc = [pick(scores, k) for k in range(EXPERTS_PER_GROUP)]
    neg = jnp.full_like(in_sel[0], -jnp.inf)

    def argmax4(vals):
        idx = jnp.zeros_like(best)
        val = vals[0]
        for k in range(1, EXPERTS_PER_GROUP):
            upd = vals[k] > val
            idx = jnp.where(upd, k, idx)
            val = jnp.where(upd, vals[k], val)
        return idx

    i1 = argmax4(in_sel)
    i2 = argmax4([jnp.where(i1 == k, neg, in_sel[k]) for k in range(EXPERTS_PER_GROUP)])

    def take(vals, idx):
        out = vals[0]
        for k in range(1, EXPERTS_PER_GROUP):
            out = jnp.where(idx == k, vals[k], out)
        return out

    w1 = take(in_sc, i1)
    w2 = take(in_sc, i2)
    den = w1 + w2
    return (best * EXPERTS_PER_GROUP + i1, best * EXPERTS_PER_GROUP + i2,
            ROUTED_SCALE * w1 / den, ROUTED_SCALE * w2 / den)


def _merge_kernel(x_ref, h_ref, ya_ref, yb_ref, cg_ref, rf_ref, rb_ref, mod_ref,
                  wg_ref, wbr_ref, wo_ref, lng_ref, lnb_ref, wrt_ref, rbias_ref,
                  x1_ref, f_ref, re_ref, rw_ref, rr_ref, *, alpha):
    h = h_ref[...]
    yc = (cg_ref[...] * (rf_ref[0] + rb_ref[0])).astype(BF16)
    ys = (ya_ref[...], yb_ref[...], yc)
    mix = None
    for kb in range(N_BRANCH):
        gate = _sigmoid(_dot(h, wg_ref[0, :, kb * D:(kb + 1) * D]))
        term = gate * _dot(ys[kb], wbr_ref[0, kb])
        mix = term if mix is None else mix + term
    y = _dot(mix.astype(BF16), wo_ref[0])

    row_i = lax.broadcasted_iota(jnp.int32, (TILE, TILE), 0)
    col_i = lax.broadcasted_iota(jnp.int32, (TILE, TILE), 1)
    tri = jnp.where(row_i <= col_i, 1.0, 0.0).astype(BF16)
    for sub in range(MM_TILE // TILE):
        rows = slice(sub * TILE, (sub + 1) * TILE)
        m = mod_ref[sub]
        g1 = m[:, 2 * D:3 * D]
        sh2 = m[:, 3 * D:4 * D]
        sc2 = m[:, 4 * D:5 * D]
        x1 = _layer_norm(alpha * x_ref[rows, :] + g1 * y[rows, :], lng_ref[0], lnb_ref[0])
        x1_ref[rows, :] = x1
        f = x1 * (1.0 + sc2) + sh2
        f_ref[rows, :] = f

        logits = lax.dot_general(wrt_ref[...], f, (((1,), (1,)), ((), ())),
                                 precision=lax.Precision.HIGHEST, preferred_element_type=F32)
        sc = jax.nn.sigmoid(logits)
        sl = sc + rbias_ref[...]
        scores = [sc[e:e + 1, :] for e in range(N_EXPERTS)]
        sel = [sl[e:e + 1, :] for e in range(N_EXPERTS)]
        e1, e2, w1, w2 = _route(scores, sel)
        re_ref[:, rows] = jnp.concatenate([e1, e2], axis=0)
        rw_ref[:, rows] = jnp.concatenate([w1, w2], axis=0)
        ranks = []
        for ek in (e1, e2):
            onehot = jnp.concatenate([jnp.where(ek == e, 1.0, 0.0) for e in range(N_EXPERTS)], axis=0)
            prefix = _dot(onehot.astype(BF16), tri)
            ranks.append(jnp.sum(onehot * prefix, axis=0, keepdims=True) - 1.0)
        rr_ref[:, rows] = jnp.concatenate(ranks, axis=0).astype(jnp.int32)


def _merge_call(l, x, h, ya, yb, cg, rec, mod_tiles, w, wrt, rbias, alpha):
    n = x.shape[0]
    tok = lambda width: pl.BlockSpec((MM_TILE, width), lambda i: (i, 0))
    names = ["wg", "w_br", "w_o", "ln1_g", "ln1_b"]
    return pl.pallas_call(
        functools.partial(_merge_kernel, alpha=alpha),
        grid=(n // MM_TILE,),
        in_specs=[
            tok(D), tok(D), tok(A_WIDTH), tok(HEADS * V_DIM), tok(C_WIDTH),
            pl.BlockSpec((1, MM_TILE, C_WIDTH), lambda i: (0, i, 0)),
            pl.BlockSpec((1, MM_TILE, C_WIDTH), lambda i: (1, i, 0)),
            pl.BlockSpec((MM_TILE // TILE, 1, 6 * D), lambda i: (i, 0, 0)),
        ] + [_const_spec(w[nm], l) for nm in names] + [
            pl.BlockSpec(wrt.shape, lambda i: (0, 0)),
            pl.BlockSpec(rbias.shape, lambda i: (0, 0)),
        ],
        out_specs=[tok(D), tok(D)] + [pl.BlockSpec((2, MM_TILE), lambda i: (0, i))] * 3,
        out_shape=[jax.ShapeDtypeStruct((n, D), F32), jax.ShapeDtypeStruct((n, D), F32),
                   jax.ShapeDtypeStruct((2, n), jnp.int32), jax.ShapeDtypeStruct((2, n), F32),
                   jax.ShapeDtypeStruct((2, n), jnp.int32)],
        compiler_params=_cparams(("parallel",)),
        name="merge",
    )(x, h, ya, yb, cg, rec, rec, mod_tiles, w["wg"], w["w_br"], w["w_o"], w["ln1_g"], w["ln1_b"],
      wrt, rbias)


def _row_gather_copy(src_hbm, row, dst, dst_row, sem):
    return pltpu.make_async_copy(src_hbm.at[pl.ds(row, 1), :], dst.at[pl.ds(dst_row, 1), :], sem)


def _tile_wait(src_hbm, dst, sem):
    pltpu.make_async_copy(src_hbm.at[pl.ds(0, dst.shape[0]), :], dst, sem).wait()


def _expert_kernel(src_ref, texp_ref, nvalid_ref, f_hbm, wgt_ref, wup_ref, wdn_ref,
                   o_ref, buf, xin_s, sem):
    i = pl.program_id(0)
    last = pl.num_programs(0) - 1
    slot = i % 2

    def issue(tile, sl):
        base = tile * MOE_TILE
        for r in range(MOE_TILE):
            _row_gather_copy(f_hbm, src_ref[base + r], buf.at[sl], r, sem.at[sl]).start()

    @pl.when(i == 0)
    def _():
        issue(0, 0)

    _tile_wait(f_hbm, buf.at[slot], sem.at[slot])
    xin_s[...] = buf[slot].astype(BF16)
    issue(jnp.where(i + 1 < nvalid_ref[0], i + 1, 0), 1 - slot)
    xin = xin_s[...]
    gt = _dot(xin, wgt_ref[0, 0])
    t = (gt * _sigmoid(gt) * _dot(xin, wup_ref[0, 0])).astype(BF16)
    o_ref[...] = _dot(t, wdn_ref[0, 0])

    @pl.when(i == last)
    def _():
        _tile_wait(f_hbm, buf.at[1 - slot], sem.at[1 - slot])


def _expert_call(l, f, src_tok, tile_expert, nvalid, w):
    ntile = tile_expert.shape[0]
    wspec = lambda a, b: pl.BlockSpec((1, 1, a, b), lambda i, src, texp, nv: (l, texp[i], 0, 0))
    return pl.pallas_call(
        _expert_kernel,
        grid_spec=pltpu.PrefetchScalarGridSpec(
            num_scalar_prefetch=3,
            grid=(ntile,),
            in_specs=[pl.BlockSpec(memory_space=pl.ANY),
                      wspec(D, D_EXPERT), wspec(D, D_EXPERT), wspec(D_EXPERT, D)],
            out_specs=pl.BlockSpec((MOE_TILE, D), lambda i, src, texp, nv: (i, 0)),
            scratch_shapes=[pltpu.VMEM((2, MOE_TILE, D), F32), pltpu.VMEM((MOE_TILE, D), BF16),
                            pltpu.SemaphoreType.DMA((2,))],
        ),
        out_shape=jax.ShapeDtypeStruct((ntile * MOE_TILE, D), F32),
        compiler_params=_cparams(("arbitrary",)),
        name="expert",
    )(src_tok, tile_expert, nvalid, f, w["w_gate"], w["w_up"], w["w_down"])


def _combine_kernel(pos_ref, ys_hbm, rw_ref, x1_ref, mod_ref, lng_ref, lnb_ref, o_ref, buf, sem,
                    *, alpha, n):
    i = pl.program_id(0)
    slot = i % 2

    def issue(tile, sl):
        for k in range(2):
            base = k * n + tile * TILE
            for r in range(TILE):
                _row_gather_copy(ys_hbm, pos_ref[base + r], buf.at[sl, k], r, sem.at[sl]).start()

    @pl.when(i == 0)
    def _():
        issue(0, 0)

    @pl.when(i + 1 < pl.num_programs(0))
    def _():
        issue(i + 1, 1 - slot)

    for k in range(2):
        _tile_wait(ys_hbm, buf.at[slot, k], sem.at[slot])
    rw = rw_ref[...]
    fo = rw[:, 0:1] * buf[slot, 0] + rw[:, 1:2] * buf[slot, 1]
    g2 = mod_ref[0, :, 5 * D:6 * D]
    o_ref[...] = _layer_norm(alpha * x1_ref[...] + g2 * fo, lng_ref[0], lnb_ref[0])


def _combine_call(l, pos, ys, rw_t, x1, mod_tiles, w, alpha):
    n = x1.shape[0]
    nt = n // TILE
    tok = lambda width: pl.BlockSpec((TILE, width), lambda i, p: (i, 0))
    lay3 = lambda a: pl.BlockSpec((1,) + a.shape[1:], lambda i, p: (l, 0, 0))
    return pl.pallas_call(
        functools.partial(_combine_kernel, alpha=alpha, n=n),
        grid_spec=pltpu.PrefetchScalarGridSpec(
            num_scalar_prefetch=1,
            grid=(nt,),
            in_specs=[pl.BlockSpec(memory_space=pl.ANY), tok(2), tok(D),
                      pl.BlockSpec((1, 1, 6 * D), lambda i, p: (i, 0, 0)),
                      lay3(w["ln2_g"]), lay3(w["ln2_b"])],
            out_specs=tok(D),
            scratch_shapes=[pltpu.VMEM((2, 2, TILE, D), F32), pltpu.SemaphoreType.DMA((2,))],
        ),
        out_shape=jax.ShapeDtypeStruct((n, D), F32),
        compiler_params=_cparams(("arbitrary",)),
        name="combine",
    )(pos, ys, rw_t, x1, mod_tiles, w["ln2_g"], w["ln2_b"])


def _invert_kernel(pos_ref, pad_lo_ref, pad_hi_ref, src_ref):
    ntok = pos_ref.shape[0] // 2
    for e in range(pad_lo_ref.shape[0]):
        def clear(i, c):
            src_ref[i] = 0
            return c

        lax.fori_loop(pad_lo_ref[e], pad_hi_ref[e], clear, 0)
    for k in range(2):
        def place(t, c, k=k):
            src_ref[pos_ref[k * ntok + t]] = t
            return c

        lax.fori_loop(0, ntok, place, 0, unroll=16)


def _invert_call(pos, pad_lo, pad_hi, nslots):
    smem = pl.BlockSpec(memory_space=pltpu.SMEM)
    return pl.pallas_call(
        _invert_kernel,
        in_specs=[smem, smem, smem],
        out_specs=smem,
        out_shape=jax.ShapeDtypeStruct((nslots,), jnp.int32),
        name="invert",
    )(pos, pad_lo, pad_hi)


def _dispatch_plan(re, rr, nt):
    n = re.shape[1]
    onehot = re.reshape(2, nt, TILE, 1) == jnp.arange(N_EXPERTS, dtype=jnp.int32)
    counts = jnp.sum(onehot, axis=2, dtype=jnp.int32)
    flat = jnp.swapaxes(counts, 0, 1).reshape(nt * 2, N_EXPERTS)
    incl = jnp.cumsum(flat, axis=0)
    total = incl[-1]
    off = jnp.swapaxes((incl - flat).reshape(nt, 2, N_EXPERTS), 0, 1)
    padded = (total + MOE_TILE - 1) // MOE_TILE * MOE_TILE
    ends = jnp.cumsum(padded)
    base = ends - padded
    start = (base + off)[:, :, None, :]
    pos = jnp.sum(jnp.where(onehot, start, 0), axis=-1).reshape(2, n) + rr
    ntile = (2 * n) // MOE_TILE + N_EXPERTS
    tile_start = jnp.arange(ntile, dtype=jnp.int32) * MOE_TILE
    tile_expert = jnp.minimum(jnp.sum(ends[None, :] <= tile_start[:, None], axis=1), N_EXPERTS - 1)
    nvalid = (ends[-1] // MOE_TILE).reshape(1)
    pos = pos.reshape(-1).astype(jnp.int32)
    nslots = ntile * MOE_TILE
    pad_lo = jnp.concatenate([base + total, ends[-1:]]).astype(jnp.int32)
    pad_hi = jnp.concatenate([ends, jnp.full((1,), nslots, ends.dtype)]).astype(jnp.int32)
    src_tok = _invert_call(pos, pad_lo, pad_hi, nslots)
    return pos, src_tok, tile_expert.astype(jnp.int32), nvalid.astype(jnp.int32)


def _rope_tables(seq, ctx_len):
    rows = seq // GRID_W
    row = jnp.repeat(jnp.arange(rows, dtype=F32), GRID_W)
    col = jnp.tile(jnp.arange(GRID_W, dtype=F32), rows)
    n_freq = QK_ROPE // 4
    inv = ROPE_BASE ** (-jnp.arange(n_freq, dtype=F32) / n_freq)
    ang = jnp.stack([row[:, None] * inv, col[:, None] * inv], axis=1)
    cos, sin = jnp.cos(ang), jnp.sin(ang)
    cos32 = jnp.broadcast_to(cos[:, :, None, :], (seq, 2, 2, n_freq)).reshape(seq, QK_ROPE)
    sign = jnp.array([-1.0, 1.0], F32)[None, None, :, None]
    sin32 = (jnp.broadcast_to(sin[:, :, None, :], (seq, 2, 2, n_freq)) * sign).reshape(seq, QK_ROPE)
    t = ctx_len + seq
    cos_t = jnp.ones((t, HEAD_PAD), F32).at[ctx_len:, ROPE_LANE0:ROPE_LANE0 + QK_ROPE].set(cos32)
    sin_t = jnp.zeros((t, HEAD_PAD), F32).at[ctx_len:, ROPE_LANE0:ROPE_LANE0 + QK_ROPE].set(sin32)
    return cos_t, sin_t


def _swap_perm():
    n_freq = QK_ROPE // 4
    r = np.arange(QK_ROPE)
    axis, half, f = r // (2 * n_freq), (r // n_freq) % 2, r % n_freq
    return axis * 2 * n_freq + (1 - half) * n_freq + f


def _pack_weights(w_in, w_uq, w_ukv, w_s, b_s, w_r, b_r, w_i, b_i, lru_lambda):
    depth = w_in.shape[0]
    perm = _swap_perm()
    dq = QK_NOPE + QK_ROPE
    wq = w_uq.reshape(depth, Q_RANK, HEADS, dq)
    zq = jnp.zeros((depth, Q_RANK, HEADS, HEAD_PAD - dq), w_uq.dtype)
    main = jnp.concatenate([wq, zq], axis=-1)
    swp = jnp.concatenate([jnp.zeros_like(wq[..., :QK_NOPE]), wq[..., QK_NOPE + perm], zq], axis=-1)
    wuq = jnp.concatenate([main.reshape(depth, Q_RANK, -1), swp.reshape(depth, Q_RANK, -1)], axis=-1)
    wkv = w_ukv.reshape(depth, KV_RANK, HEADS, QK_NOPE + V_DIM)
    wuk = jnp.concatenate([wkv[..., :QK_NOPE], jnp.zeros((depth, KV_RANK, HEADS, HEAD_PAD - QK_NOPE), w_ukv.dtype)],
                          axis=-1).reshape(depth, KV_RANK, -1)
    wuv = wkv[..., QK_NOPE:].reshape(depth, KV_RANK, -1)
    wkr = w_in[:, :, OFF_KR:OFF_KR + QK_ROPE]
    zl = jnp.zeros((depth, D, ROPE_LANE0), w_in.dtype)
    zr = jnp.zeros((depth, D, HEAD_PAD - ROPE_LANE0 - QK_ROPE), w_in.dtype)
    wb = jnp.concatenate([w_in[:, :, OFF_CQ:OFF_KR], zl, wkr, zr, zl, wkr[:, :, perm], zr], axis=-1)
    eye = jnp.eye(C_BLOCKS, dtype=w_r.dtype)

    def dense(wblk):
        full = jnp.einsum("ldhij,hk->ldhikj", wblk, eye)
        return full.reshape(depth, 2, C_WIDTH, C_WIDTH)

    wri = jnp.concatenate([dense(w_r), dense(w_i)], axis=-1).reshape(depth * 2, C_WIDTH, 2 * C_WIDTH)
    bri = jnp.concatenate([b_r, b_i], axis=-1).reshape(depth * 2, 1, 2 * C_WIDTH)
    gw = A_WIDTH // A_GROUPS
    bs_full = jnp.broadcast_to(jnp.swapaxes(b_s, 1, 2)[:, :, :, None],
                               (depth, CHUNK, A_GROUPS, gw)).reshape(depth, CHUNK, A_WIDTH)
    return {
        "wa": w_in[:, :, OFF_AU:OFF_CQ].astype(BF16),
        "wb": wb.astype(BF16),
        "wc": w_in[:, :, OFF_CX:OFF_GATE].astype(BF16),
        "wg": w_in[:, :, OFF_GATE:].astype(BF16),
        "wuq": wuq.astype(BF16), "wuk": wuk.astype(BF16), "wuv": wuv.astype(BF16),
        "w_s": w_s.astype(BF16), "bs_full": bs_full,
        "wri": wri.astype(BF16), "bri": bri,
        "lam": lru_lambda.reshape(depth * 2, 1, C_WIDTH),
    }


def kernel(x, c, ctx, c_ctx, w_ada, b_ada, w_in, a_ln_g, a_ln_b, w_s, b_s, q_norm_g, w_uq,
           kv_norm_g, w_ukv, conv_w, conv_b, w_r, b_r, w_i, b_i, lru_lambda, w_br, w_o,
           ln1_g, ln1_b, w_router, router_bias, w_gate, w_up, w_down, ln2_g, ln2_b):
    depth = w_in.shape[0]
    bsz, seq, _ = x.shape
    ctx_len = ctx.shape[1]
    assert ctx_len == TILE and seq % TILE == 0 and bsz == SUBLANES
    t = ctx_len + seq
    nct = t // TILE
    n = bsz * t
    assert n % MM_TILE == 0 and (2 * n) % MOE_TILE == 0
    alpha = (2.0 * depth) ** 0.25

    row3 = lambda a: a.reshape(depth, 1, a.shape[-1])
    w = _pack_weights(w_in, w_uq, w_ukv, w_s, b_s, w_r, b_r, w_i, b_i, lru_lambda)
    w.update({
        "a_ln_g": row3(a_ln_g), "a_ln_b": row3(a_ln_b),
        "q_norm_g": row3(q_norm_g), "kv_norm_g": row3(kv_norm_g),
        "conv_w": conv_w, "conv_b": row3(conv_b),
        "w_br": w_br.astype(BF16), "w_o": w_o.astype(BF16),
        "ln1_g": row3(ln1_g), "ln1_b": row3(ln1_b), "ln2_g": row3(ln2_g), "ln2_b": row3(ln2_b),
        "w_gate": w_gate.astype(BF16), "w_up": w_up.astype(BF16), "w_down": w_down.astype(BF16),
    })
    cos_t, sin_t = _rope_tables(seq, ctx_len)
    wrt = w_router.T
    rbias = router_bias.reshape(N_EXPERTS, 1)

    cond_rows = jnp.zeros((2 * SUBLANES, D), F32).at[:bsz].set(c).at[bsz].set(c_ctx)
    mod = _mod_call(cond_rows, w_ada, b_ada)
    tile_row = np.array([bsz if j == 0 else b for b in range(bsz) for j in range(nct)], np.int32)
    mod_tiles = mod[:, tile_row, :].reshape(depth, n // TILE, 1, 6 * D)

    xs = jnp.concatenate([ctx, x], axis=1).reshape(n, D)
    for l in range(depth):
        mt = mod_tiles[l]
        h, ya, q, k, v, xc, cg = _proj_call(l, xs, mt, cos_t, sin_t, w, nct)
        yb = _attn_call(q, k, v, bsz, nct)
        rec_tm = _scan_call(l, jnp.swapaxes(xc.reshape(bsz, t, C_WIDTH), 0, 1), w, nct)
        rec = jnp.swapaxes(rec_tm, 1, 2).reshape(2, n, C_WIDTH)
        x1, f, re, rw, rr = _merge_call(l, xs, h, ya, yb, cg, rec, mt, w, wrt, rbias, alpha)
        pos, src_tok, tile_expert, nvalid = _dispatch_plan(re, rr, n // TILE)
        ys = _expert_call(l, f, src_tok, tile_expert, nvalid, w)
        xs = _combine_call(l, pos, ys, rw.T, x1, mt, w, alpha)
    return xs.reshape(bsz, t, D)[:, ctx_len:, :]
```

```python
import functools

import numpy as np
import jax
import jax.numpy as jnp
from jax import lax
from jax.experimental import pallas as pl
from jax.experimental.pallas import tpu as pltpu

D = 1024
GRID_W = 64
A_WIDTH = 512
A_GROUPS = 4
CHUNK = 128
HEADS = 8
QK_NOPE = 64
QK_ROPE = 32
V_DIM = 64
Q_RANK = 256
KV_RANK = 128
ROPE_BASE = 10000.0
C_WIDTH = 512
C_BLOCKS = 8
C_BLOCK_W = C_WIDTH // C_BLOCKS
CONV_W = 4
CONV_LEFT = 2
LRU_C = 8.0
N_BRANCH = 3
BRANCH_W = 512
OFF_AU = 0
OFF_AV = OFF_AU + A_WIDTH
OFF_CQ = OFF_AV + A_WIDTH
OFF_CKV = OFF_CQ + Q_RANK
OFF_KR = OFF_CKV + KV_RANK
OFF_CX = OFF_KR + QK_ROPE
OFF_CG = OFF_CX + C_WIDTH
OFF_GATE = OFF_CG + C_WIDTH
N_EXPERTS = 16
N_GROUPS = 4
EXPERTS_PER_GROUP = N_EXPERTS // N_GROUPS
D_EXPERT = 512
ROUTED_SCALE = 2.5
EPS = 1e-6

LANES = 128
SUBLANES = 8
TILE = 256
HEAD_PAD = 128
ROPE_LANE0 = QK_NOPE
MOE_TILE = 256
ATT_Q = 128
ATT_K = 2 * LANES
MM_TILE = 512
VMEM_LIMIT = 56 * 1024 * 1024

BF16 = jnp.bfloat16
F32 = jnp.float32


def _cparams(sem):
    return pltpu.CompilerParams(dimension_semantics=sem, vmem_limit_bytes=VMEM_LIMIT)


def _dot(a, b):
    return jnp.dot(a, b, preferred_element_type=F32)


def _sigmoid(v):
    return 0.5 * jnp.tanh(0.5 * v) + 0.5


def _layer_norm(v, g, b):
    mu = jnp.mean(v, axis=-1, keepdims=True)
    c = v - mu
    var = jnp.mean(c * c, axis=-1, keepdims=True)
    return (c * lax.rsqrt(var + EPS)) * g + b


def _rms_norm(v, g):
    ms = jnp.mean(v * v, axis=-1, keepdims=True)
    return (v * lax.rsqrt(ms + EPS)) * g


def _mod_kernel(c_ref, w_ref, b_ref, o_ref):
    c = c_ref[...]
    cond = (c * jax.nn.sigmoid(c)).astype(BF16)
    o_ref[0] = _dot(cond, w_ref[0].astype(BF16)) + b_ref[0]


def _mod_call(cond_rows, w_ada, b_ada):
    depth = w_ada.shape[0]
    rows = cond_rows.shape[0]
    ncol = 6 * D
    bn = 1536
    return pl.pallas_call(
        _mod_kernel,
        grid=(depth, ncol // bn),
        in_specs=[
            pl.BlockSpec((rows, D), lambda l, j: (0, 0)),
            pl.BlockSpec((1, D, bn), lambda l, j: (l, 0, j)),
            pl.BlockSpec((1, 1, bn), lambda l, j: (l, 0, j)),
        ],
        out_specs=pl.BlockSpec((1, rows, bn), lambda l, j: (l, 0, j)),
        out_shape=jax.ShapeDtypeStruct((depth, rows, ncol), F32),
        compiler_params=_cparams(("parallel", "parallel")),
        name="mod",
    )(cond_rows, w_ada, b_ada.reshape(depth, 1, ncol))


def _proj_kernel(x_ref, mod_ref, cos0_ref, cos1_ref, sin0_ref, sin1_ref, wa_ref, wb_ref, wc_ref,
                 alng_ref, alnb_ref, ws_ref, bs_ref, qg_ref, wuq_ref, kvg_ref,
                 wuk_ref, wuv_ref,
                 h_ref, ya_ref, q_ref, k_ref, v_ref, xc_ref, cg_ref):
    halves = []
    for sub in range(MM_TILE // TILE):
        rows = slice(sub * TILE, (sub + 1) * TILE)
        m = mod_ref[sub]
        halves.append((x_ref[rows, :] * (1.0 + m[:, D:2 * D]) + m[:, 0:D]).astype(BF16))
    h = jnp.concatenate(halves, axis=0)
    h_ref[...] = h

    uv = _dot(h, wa_ref[0])
    u = jax.nn.gelu(uv[:, :A_WIDTH])
    v = _layer_norm(jax.nn.gelu(uv[:, A_WIDTH:]), alng_ref[0], alnb_ref[0]).astype(BF16)
    gw = A_WIDTH // A_GROUPS
    for c in range(MM_TILE // CHUNK):
        rows = slice(c * CHUNK, (c + 1) * CHUNK)
        for g in range(A_GROUPS):
            cols = slice(g * gw, (g + 1) * gw)
            mixed = _dot(ws_ref[0, g], v[rows, cols]) + bs_ref[0, :, cols]
            ya_ref[rows, cols] = (u[rows, cols] * mixed).astype(BF16)

    pb = _dot(h, wb_ref[0])
    cos_t = jnp.concatenate([cos0_ref[...], cos1_ref[...]], axis=0)
    sin_t = jnp.concatenate([sin0_ref[...], sin1_ref[...]], axis=0)
    cqn = _rms_norm(pb[:, :Q_RANK], qg_ref[0]).astype(BF16)
    qq = _dot(cqn, wuq_ref[0])
    hw = HEADS * HEAD_PAD
    for hd in range(HEADS):
        a = slice(hd * HEAD_PAD, (hd + 1) * HEAD_PAD)
        b = slice(hw + hd * HEAD_PAD, hw + (hd + 1) * HEAD_PAD)
        q_ref[:, a] = (qq[:, a] * cos_t + qq[:, b] * sin_t).astype(BF16)
    ckvn = _rms_norm(pb[:, Q_RANK:Q_RANK + KV_RANK], kvg_ref[0]).astype(BF16)
    o_kr = Q_RANK + KV_RANK
    kr = pb[:, o_kr:o_kr + HEAD_PAD] * cos_t + pb[:, o_kr + HEAD_PAD:o_kr + 2 * HEAD_PAD] * sin_t
    kk = _dot(ckvn, wuk_ref[0])
    for hd in range(HEADS):
        a = slice(hd * HEAD_PAD, (hd + 1) * HEAD_PAD)
        k_ref[:, a] = (kk[:, a] + kr).astype(BF16)
    v_ref[...] = _dot(ckvn, wuv_ref[0]).astype(BF16)

    pc = _dot(h, wc_ref[0])
    xc_ref[...] = pc[:, :C_WIDTH]
    cg_ref[...] = jax.nn.gelu(pc[:, C_WIDTH:])


def _const_spec(a, l):
    nz = (0,) * (a.ndim - 1)
    return pl.BlockSpec((1,) + a.shape[1:], lambda i: (l,) + nz, pipeline_mode=pl.Buffered(1))


def _proj_call(l, x, mod_tiles, cos_t, sin_t, w, nct):
    n = x.shape[0]
    nsub = MM_TILE // TILE
    assert nsub == 2
    tok = lambda width: pl.BlockSpec((MM_TILE, width), lambda i: (i, 0))
    pos0 = pl.BlockSpec((TILE, HEAD_PAD), lambda i: ((nsub * i) % nct, 0))
    pos1 = pl.BlockSpec((TILE, HEAD_PAD), lambda i: ((nsub * i + 1) % nct, 0))
    outs = [
        (D, BF16), (A_WIDTH, BF16), (HEADS * HEAD_PAD, BF16), (HEADS * HEAD_PAD, BF16),
        (HEADS * V_DIM, BF16), (C_WIDTH, F32), (C_WIDTH, F32),
    ]
    names = ["wa", "wb", "wc", "a_ln_g", "a_ln_b", "w_s", "bs_full", "q_norm_g", "wuq",
             "kv_norm_g", "wuk", "wuv"]
    return pl.pallas_call(
        _proj_kernel,
        grid=(n // MM_TILE,),
        in_specs=[
            tok(D),
            pl.BlockSpec((nsub, 1, 6 * D), lambda i: (i, 0, 0)),
            pos0, pos1, pos0, pos1,
        ] + [_const_spec(w[nm], l) for nm in names],
        out_specs=[tok(wd) for wd, _ in outs],
        out_shape=[jax.ShapeDtypeStruct((n, wd), dt) for wd, dt in outs],
        compiler_params=_cparams(("parallel",)),
        name="proj",
    )(x, mod_tiles, cos_t, cos_t, sin_t, sin_t, *[w[nm] for nm in names])


def _attn_kernel(q_ref, k_ref, v_ref, o_ref, s_buf, vext):
    j = pl.program_id(2)
    c2 = float(QK_NOPE + QK_ROPE) ** -0.5 * float(np.log2(np.e))
    nt_dims = (((1,), (1,)), ((), ()))
    nv = 2 * V_DIM

    @pl.when(j == 0)
    def _():
        vext[:, :nv] = v_ref[...]
        vext[:, nv:] = jnp.ones((vext.shape[0], vext.shape[1] - nv), BF16)

    def attend(nk):
        for qh in range(TILE // ATT_Q):
            rows = slice(qh * ATT_Q, (qh + 1) * ATT_Q)
            for hd in range(2):
                cols = slice(hd * HEAD_PAD, (hd + 1) * HEAD_PAD)
                q = q_ref[rows, cols]
                m = None
                for c in range(nk // ATT_K):
                    keys = slice(c * ATT_K, (c + 1) * ATT_K)
                    s = lax.dot_general(q, k_ref[keys, cols], nt_dims, preferred_element_type=F32)
                    s_buf[:, keys] = s
                    sm = jnp.maximum(s[:, :LANES], s[:, LANES:])
                    m = sm if m is None else jnp.maximum(m, sm)
                mx = jnp.max(m, axis=-1, keepdims=True)
                acc = None
                for c in range(nk // ATT_K):
                    keys = slice(c * ATT_K, (c + 1) * ATT_K)
                    p = jnp.exp2((s_buf[:, keys] - mx) * c2)
                    pv = _dot(p.astype(BF16), vext[keys, :])
                    acc = pv if acc is None else acc + pv
                out = (acc[:, :nv] / acc[:, nv:]).astype(BF16)
                vcols = slice(hd * V_DIM, (hd + 1) * V_DIM)
                o_ref[rows, vcols] = out[:, vcols]

    @pl.when(j == 0)
    def _():
        attend(TILE)

    @pl.when(j > 0)
    def _():
        attend(k_ref.shape[0])


def _attn_call(q, k, v, bsz, nct):
    n = q.shape[0]
    t = nct * TILE
    npair = HEADS // 2
    return pl.pallas_call(
        _attn_kernel,
        grid=(bsz, npair, nct),
        in_specs=[
            pl.BlockSpec((TILE, 2 * HEAD_PAD), lambda b, p, j: (b * nct + j, p)),
            pl.BlockSpec((t, 2 * HEAD_PAD), lambda b, p, j: (b, p)),
            pl.BlockSpec((t, 2 * V_DIM), lambda b, p, j: (b, p)),
        ],
        out_specs=pl.BlockSpec((TILE, 2 * V_DIM), lambda b, p, j: (b * nct + j, p)),
        out_shape=jax.ShapeDtypeStruct((n, HEADS * V_DIM), BF16),
        scratch_shapes=[pltpu.VMEM((ATT_Q, t), F32), pltpu.VMEM((t, 4 * V_DIM), BF16)],
        compiler_params=_cparams(("parallel", "parallel", "arbitrary")),
        name="attn",
    )(q, k, v)


def _scan_chunk(d, s, nct):
    return jnp.where(d == 0, s, jnp.where(s == 0, 0, nct - s))


def _scan_kernel(xc_ref, prev_ref, next_ref, cw_ref, cb_ref, wri_ref, bri_ref, lam_ref,
                 o_ref, a_s, bx_s, carry, *, nct):
    d = pl.program_id(0)
    s = pl.program_id(1)
    chunk = _scan_chunk(d, s, nct)
    bsz = xc_ref.shape[1]
    has_prev = jnp.logical_and(chunk != 0, chunk != 1)
    has_next = jnp.logical_and(chunk != 0, chunk != nct - 1)
    cur = xc_ref[...]
    left = CONV_LEFT
    right = CONV_W - 1 - CONV_LEFT
    xp = jnp.concatenate([jnp.where(has_prev, prev_ref[...], 0.0), cur,
                          jnp.where(has_next, next_ref[...], 0.0)], axis=0)
    xconv = cb_ref[...]
    for kk in range(CONV_W):
        xconv = xconv + xp[kk:kk + TILE] * cw_ref[0, kk:kk + 1, :]
    assert xp.shape[0] == TILE + left + right
    x2 = xconv.reshape(TILE * bsz, C_WIDTH)
    ri = _dot(x2.astype(BF16), wri_ref[0]) + bri_ref[0]
    r = _sigmoid(ri[:, :C_WIDTH])
    ig = _sigmoid(ri[:, C_WIDTH:])
    lam = lam_ref[0]
    log_sig = jnp.minimum(lam, 0.0) - jnp.log1p(jnp.exp(-jnp.abs(lam)))
    log_a = LRU_C * r * log_sig
    a = jnp.exp(log_a)
    a_s[...] = a
    bx_s[...] = jnp.sqrt(-jnp.tanh(log_a) * (a * a + 1.0)) * (ig * x2)

    @pl.when(s == 0)
    def _():
        carry[...] = jnp.zeros_like(carry)

    def step(t, hprev):
        tt = jnp.where(d == 0, t, TILE - 1 - t)
        rows = pl.ds(pl.multiple_of(tt * bsz, bsz), bsz)
        hnew = a_s[rows, :] * hprev + bx_s[rows, :]
        o_ref[0, tt] = hnew
        return hnew

    carry[...] = lax.fori_loop(0, TILE, step, carry[...], unroll=8)


def _scan_call(l, xc3, w, nct):
    t, bsz, _ = xc3.shape
    left = CONV_LEFT
    right = CONV_W - 1 - CONV_LEFT
    assert TILE % left == 0 and right == 1
    cur = lambda d, s: (_scan_chunk(d, s, nct), 0, 0)
    prv = lambda d, s: (jnp.maximum(_scan_chunk(d, s, nct) * (TILE // left) - 1, 0), 0, 0)
    nxt = lambda d, s: (jnp.minimum((_scan_chunk(d, s, nct) + 1) * TILE, t - 1), 0, 0)
    return pl.pallas_call(
        functools.partial(_scan_kernel, nct=nct),
        grid=(2, nct),
        in_specs=[
            pl.BlockSpec((TILE, bsz, C_WIDTH), cur),
            pl.BlockSpec((left, bsz, C_WIDTH), prv),
            pl.BlockSpec((right, bsz, C_WIDTH), nxt),
            pl.BlockSpec((1, CONV_W, C_WIDTH), lambda d, s: (l, 0, 0)),
            pl.BlockSpec((1, 1, C_WIDTH), lambda d, s: (l, 0, 0)),
            pl.BlockSpec((1, C_WIDTH, 2 * C_WIDTH), lambda d, s: (2 * l + d, 0, 0)),
            pl.BlockSpec((1, 1, 2 * C_WIDTH), lambda d, s: (2 * l + d, 0, 0)),
            pl.BlockSpec((1, 1, C_WIDTH), lambda d, s: (2 * l + d, 0, 0)),
        ],
        out_specs=pl.BlockSpec((1, TILE, bsz, C_WIDTH),
                               lambda d, s: (d, _scan_chunk(d, s, nct), 0, 0)),
        out_shape=jax.ShapeDtypeStruct((2, t, bsz, C_WIDTH), F32),
        scratch_shapes=[
            pltpu.VMEM((TILE * bsz, C_WIDTH), F32),
            pltpu.VMEM((TILE * bsz, C_WIDTH), F32),
            pltpu.VMEM((bsz, C_WIDTH), F32),
        ],
        compiler_params=_cparams(("arbitrary", "arbitrary")),
        name="scan",
    )(xc3, xc3, xc3, w["conv_w"], w["conv_b"], w["wri"], w["bri"], w["lam"])


def _top2_sum(v):
    hi1, lo1 = jnp.maximum(v[0], v[1]), jnp.minimum(v[0], v[1])
    hi2, lo2 = jnp.maximum(v[2], v[3]), jnp.minimum(v[2], v[3])
    top = jnp.maximum(hi1, hi2)
    second = jnp.maximum(jnp.minimum(hi1, hi2), jnp.maximum(lo1, lo2))
    return top + second


def _route(scores, sel):
    gs = [_top2_sum(sel[g * EXPERTS_PER_GROUP:(g + 1) * EXPERTS_PER_GROUP]) for g in range(N_GROUPS)]
    best = jnp.zeros_like(gs[0], dtype=jnp.int32)
    best_v = gs[0]
    for g in range(1, N_GROUPS):
        upd = gs[g] > best_v
        best = jnp.where(upd, g, best)
        best_v = jnp.where(upd, gs[g], best_v)

    def pick(rows, k):
        out = rows[k]
        for g in range(1, N_GROUPS):
            out = jnp.where(best == g, rows[g * EXPERTS_PER_GROUP + k], out)
        return out

    in_sel = [pick(sel, k) for k in range(EXPERTS_PER_GROUP)]
    in_sc = [pick(scores, k) for k in range(EXPERTS_PER_GROUP)]
    neg = jnp.full_like(in_sel[0], -jnp.inf)

    def argmax4(vals):
        idx = jnp.zeros_like(best)
        val = vals[0]
        for k in range(1, EXPERTS_PER_GROUP):
            upd = vals[k] > val
            idx = jnp.where(upd, k, idx)
            val = jnp.where(upd, vals[k], val)
        return idx

    i1 = argmax4(in_sel)
    i2 = argmax4([jnp.where(i1 == k, neg, in_sel[k]) for k in range(EXPERTS_PER_GROUP)])

    def take(vals, idx):
        out = vals[0]
        for k in range(1, EXPERTS_PER_GROUP):
            out = jnp.where(idx == k, vals[k], out)
        return out

    w1 = take(in_sc, i1)
    w2 = take(in_sc, i2)
    den = w1 + w2
    return (best * EXPERTS_PER_GROUP + i1, best * EXPERTS_PER_GROUP + i2,
            ROUTED_SCALE * w1 / den, ROUTED_SCALE * w2 / den)


def _merge_kernel(x_ref, h_ref, ya_ref, yb_ref, cg_ref, rf_ref, rb_ref, mod_ref,
                  wg_ref, wbr_ref, wo_ref, lng_ref, lnb_ref, wrt_ref, rbias_ref,
                  x1_ref, f_ref, re_ref, rw_ref, rr_ref, *, alpha):
    h = h_ref[...]
    yc = (cg_ref[...] * (rf_ref[0] + rb_ref[0])).astype(BF16)
    ys = (ya_ref[...], yb_ref[...], yc)
    mix = None
    for kb in range(N_BRANCH):
        gate = _sigmoid(_dot(h, wg_ref[0, :, kb * D:(kb + 1) * D]))
        term = gate * _dot(ys[kb], wbr_ref[0, kb])
        mix = term if mix is None else mix + term
    y = _dot(mix.astype(BF16), wo_ref[0])

    row_i = lax.broadcasted_iota(jnp.int32, (TILE, TILE), 0)
    col_i = lax.broadcasted_iota(jnp.int32, (TILE, TILE), 1)
    tri = jnp.where(row_i <= col_i, 1.0, 0.0).astype(BF16)
    for sub in range(MM_TILE // TILE):
        rows = slice(sub * TILE, (sub + 1) * TILE)
        m = mod_ref[sub]
        g1 = m[:, 2 * D:3 * D]
        sh2 = m[:, 3 * D:4 * D]
        sc2 = m[:, 4 * D:5 * D]
        x1 = _layer_norm(alpha * x_ref[rows, :] + g1 * y[rows, :], lng_ref[0], lnb_ref[0])
        x1_ref[rows, :] = x1
        f = x1 * (1.0 + sc2) + sh2
        f_ref[rows, :] = f

        logits = lax.dot_general(wrt_ref[...], f, (((1,), (1,)), ((), ())),
                                 precision=lax.Precision.HIGHEST, preferred_element_type=F32)
        sc = jax.nn.sigmoid(logits)
        sl = sc + rbias_ref[...]
        scores = [sc[e:e + 1, :] for e in range(N_EXPERTS)]
        sel = [sl[e:e + 1, :] for e in range(N_EXPERTS)]
        e1, e2, w1, w2 = _route(scores, sel)
        re_ref[:, rows] = jnp.concatenate([e1, e2], axis=0)
        rw_ref[:, rows] = jnp.concatenate([w1, w2], axis=0)
        ranks = []
        for ek in (e1, e2):
            onehot = jnp.concatenate([jnp.where(ek == e, 1.0, 0.0) for e in range(N_EXPERTS)], axis=0)
            prefix = _dot(onehot.astype(BF16), tri)
            ranks.append(jnp.sum(onehot * prefix, axis=0, keepdims=True) - 1.0)
        rr_ref[:, rows] = jnp.concatenate(ranks, axis=0).astype(jnp.int32)


def _merge_call(l, x, h, ya, yb, cg, rec, mod_tiles, w, wrt, rbias, alpha):
    n = x.shape[0]
    tok = lambda width: pl.BlockSpec((MM_TILE, width), lambda i: (i, 0))
    names = ["wg", "w_br", "w_o", "ln1_g", "ln1_b"]
    return pl.pallas_call(
        functools.partial(_merge_kernel, alpha=alpha),
        grid=(n // MM_TILE,),
        in_specs=[
            tok(D), tok(D), tok(A_WIDTH), tok(HEADS * V_DIM), tok(C_WIDTH),
            pl.BlockSpec((1, MM_TILE, C_WIDTH), lambda i: (0, i, 0)),
            pl.BlockSpec((1, MM_TILE, C_WIDTH), lambda i: (1, i, 0)),
            pl.BlockSpec((MM_TILE // TILE, 1, 6 * D), lambda i: (i, 0, 0)),
        ] + [_const_spec(w[nm], l) for nm in names] + [
            pl.BlockSpec(wrt.shape, lambda i: (0, 0)),
            pl.BlockSpec(rbias.shape, lambda i: (0, 0)),
        ],
        out_specs=[tok(D), tok(D)] + [pl.BlockSpec((2, MM_TILE), lambda i: (0, i))] * 3,
        out_shape=[jax.ShapeDtypeStruct((n, D), F32), jax.ShapeDtypeStruct((n, D), F32),
                   jax.ShapeDtypeStruct((2, n), jnp.int32), jax.ShapeDtypeStruct((2, n), F32),
                   jax.ShapeDtypeStruct((2, n), jnp.int32)],
        compiler_params=_cparams(("parallel",)),
        name="merge",
    )(x, h, ya, yb, cg, rec, rec, mod_tiles, w["wg"], w["w_br"], w["w_o"], w["ln1_g"], w["ln1_b"],
      wrt, rbias)


def _row_gather_copy(src_hbm, row, dst, dst_row, sem):
    return pltpu.make_async_copy(src_hbm.at[pl.ds(row, 1), :], dst.at[pl.ds(dst_row, 1), :], sem)


def _tile_wait(src_hbm, dst, sem):
    pltpu.make_async_copy(src_hbm.at[pl.ds(0, dst.shape[0]), :], dst, sem).wait()


def _expert_kernel(src_ref, texp_ref, nvalid_ref, f_hbm, wgt_ref, wup_ref, wdn_ref,
                   o_ref, buf, sem):
    i = pl.program_id(0)
    nvalid = nvalid_ref[0]
    slot = i % 2

    def issue(tile, sl):
        base = tile * MOE_TILE
        for r in range(MOE_TILE):
            _row_gather_copy(f_hbm, src_ref[base + r], buf.at[sl], r, sem.at[sl]).start()

    @pl.when(jnp.logical_and(i == 0, nvalid > 0))
    def _():
        issue(0, 0)

    @pl.when(i + 1 < nvalid)
    def _():
        issue(i + 1, 1 - slot)

    @pl.when(i < nvalid)
    def _():
        _tile_wait(f_hbm, buf.at[slot], sem.at[slot])
        xin = buf[slot].astype(BF16)
        gt = _dot(xin, wgt_ref[0, 0])
        t = (gt * _sigmoid(gt) * _dot(xin, wup_ref[0, 0])).astype(BF16)
        o_ref[...] = _dot(t, wdn_ref[0, 0])

    @pl.when(i >= nvalid)
    def _():
        o_ref[...] = jnp.zeros_like(o_ref)


def _expert_call(l, f, src_tok, tile_expert, nvalid, w):
    ntile = tile_expert.shape[0]
    wspec = lambda a, b: pl.BlockSpec((1, 1, a, b), lambda i, src, texp, nv: (l, texp[i], 0, 0))
    return pl.pallas_call(
        _expert_kernel,
        grid_spec=pltpu.PrefetchScalarGridSpec(
            num_scalar_prefetch=3,
            grid=(ntile,),
            in_specs=[pl.BlockSpec(memory_space=pl.ANY),
                      wspec(D, D_EXPERT), wspec(D, D_EXPERT), wspec(D_EXPERT, D)],
            out_specs=pl.BlockSpec((MOE_TILE, D), lambda i, src, texp, nv: (i, 0)),
            scratch_shapes=[pltpu.VMEM((2, MOE_TILE, D), F32), pltpu.SemaphoreType.DMA((2,))],
        ),
        out_shape=jax.ShapeDtypeStruct((ntile * MOE_TILE, D), F32),
        compiler_params=_cparams(("arbitrary",)),
        name="expert",
    )(src_tok, tile_expert, nvalid, f, w["w_gate"], w["w_up"], w["w_down"])


def _combine_kernel(pos_ref, ys_hbm, rw_ref, x1_ref, mod_ref, lng_ref, lnb_ref, o_ref, buf, sem,
                    *, alpha, n):
    i = pl.program_id(0)
    slot = i % 2

    def issue(tile, sl):
        for k in range(2):
            base = k * n + tile * TILE
            for r in range(TILE):
                _row_gather_copy(ys_hbm, pos_ref[base + r], buf.at[sl, k], r, sem.at[sl]).start()

    @pl.when(i == 0)
    def _():
        issue(0, 0)

    @pl.when(i + 1 < pl.num_programs(0))
    def _():
        issue(i + 1, 1 - slot)

    for k in range(2):
        _tile_wait(ys_hbm, buf.at[slot, k], sem.at[slot])
    rw = rw_ref[...]
    fo = rw[:, 0:1] * buf[slot, 0] + rw[:, 1:2] * buf[slot, 1]
    g2 = mod_ref[0, :, 5 * D:6 * D]
    o_ref[...] = _layer_norm(alpha * x1_ref[...] + g2 * fo, lng_ref[0], lnb_ref[0])


def _combine_call(l, pos, ys, rw_t, x1, mod_tiles, w, alpha):
    n = x1.shape[0]
    nt = n // TILE
    tok = lambda width: pl.BlockSpec((TILE, width), lambda i, p: (i, 0))
    lay3 = lambda a: pl.BlockSpec((1,) + a.shape[1:], lambda i, p: (l, 0, 0))
    return pl.pallas_call(
        functools.partial(_combine_kernel, alpha=alpha, n=n),
        grid_spec=pltpu.PrefetchScalarGridSpec(
            num_scalar_prefetch=1,
            grid=(nt,),
            in_specs=[pl.BlockSpec(memory_space=pl.ANY), tok(2), tok(D),
                      pl.BlockSpec((1, 1, 6 * D), lambda i, p: (i, 0, 0)),
                      lay3(w["ln2_g"]), lay3(w["ln2_b"])],
            out_specs=tok(D),
            scratch_shapes=[pltpu.VMEM((2, 2, TILE, D), F32), pltpu.SemaphoreType.DMA((2,))],
        ),
        out_shape=jax.ShapeDtypeStruct((n, D), F32),
        compiler_params=_cparams(("arbitrary",)),
        name="combine",
    )(pos, ys, rw_t, x1, mod_tiles, w["ln2_g"], w["ln2_b"])


def _invert_kernel(pos_ref, pad_lo_ref, pad_hi_ref, src_ref):
    ntok = pos_ref.shape[0] // 2
    for e in range(pad_lo_ref.shape[0]):
        def clear(i, c):
            src_ref[i] = 0
            return c

        lax.fori_loop(pad_lo_ref[e], pad_hi_ref[e], clear, 0)
    for k in range(2):
        def place(t, c, k=k):
            src_ref[pos_ref[k * ntok + t]] = t
            return c

        lax.fori_loop(0, ntok, place, 0, unroll=16)


def _invert_call(pos, pad_lo, pad_hi, nslots):
    smem = pl.BlockSpec(memory_space=pltpu.SMEM)
    return pl.pallas_call(
        _invert_kernel,
        in_specs=[smem, smem, smem],
        out_specs=smem,
        out_shape=jax.ShapeDtypeStruct((nslots,), jnp.int32),
        name="invert",
    )(pos, pad_lo, pad_hi)


def _dispatch_plan(re, rr, nt):
    n = re.shape[1]
    onehot = re.reshape(2, nt, TILE, 1) == jnp.arange(N_EXPERTS, dtype=jnp.int32)
    counts = jnp.sum(onehot, axis=2, dtype=jnp.int32)
    flat = jnp.swapaxes(counts, 0, 1).reshape(nt * 2, N_EXPERTS)
    incl = jnp.cumsum(flat, axis=0)
    total = incl[-1]
    off = jnp.swapaxes((incl - flat).reshape(nt, 2, N_EXPERTS), 0, 1)
    padded = (total + MOE_TILE - 1) // MOE_TILE * MOE_TILE
    ends = jnp.cumsum(padded)
    base = ends - padded
    start = (base + off)[:, :, None, :]
    pos = jnp.sum(jnp.where(onehot, start, 0), axis=-1).reshape(2, n) + rr
    ntile = (2 * n) // MOE_TILE + N_EXPERTS
    tile_start = jnp.arange(ntile, dtype=jnp.int32) * MOE_TILE
    tile_expert = jnp.minimum(jnp.sum(ends[None, :] <= tile_start[:, None], axis=1), N_EXPERTS - 1)
    nvalid = (ends[-1] // MOE_TILE).reshape(1)
    pos = pos.reshape(-1).astype(jnp.int32)
    nslots = ntile * MOE_TILE
    pad_lo = jnp.concatenate([base + total, ends[-1:]]).astype(jnp.int32)
    pad_hi = jnp.concatenate([ends, jnp.full((1,), nslots, ends.dtype)]).astype(jnp.int32)
    src_tok = _invert_call(pos, pad_lo, pad_hi, nslots)
    return pos, src_tok, tile_expert.astype(jnp.int32), nvalid.astype(jnp.int32)


def _rope_tables(seq, ctx_len):
    rows = seq // GRID_W
    row = jnp.repeat(jnp.arange(rows, dtype=F32), GRID_W)
    col = jnp.tile(jnp.arange(GRID_W, dtype=F32), rows)
    n_freq = QK_ROPE // 4
    inv = ROPE_BASE ** (-jnp.arange(n_freq, dtype=F32) / n_freq)
    ang = jnp.stack([row[:, None] * inv, col[:, None] * inv], axis=1)
    cos, sin = jnp.cos(ang), jnp.sin(ang)
    cos32 = jnp.broadcast_to(cos[:, :, None, :], (seq, 2, 2, n_freq)).reshape(seq, QK_ROPE)
    sign = jnp.array([-1.0, 1.0], F32)[None, None, :, None]
    sin32 = (jnp.broadcast_to(sin[:, :, None, :], (seq, 2, 2, n_freq)) * sign).reshape(seq, QK_ROPE)
    t = ctx_len + seq
    cos_t = jnp.ones((t, HEAD_PAD), F32).at[ctx_len:, ROPE_LANE0:ROPE_LANE0 + QK_ROPE].set(cos32)
    sin_t = jnp.zeros((t, HEAD_PAD), F32).at[ctx_len:, ROPE_LANE0:ROPE_LANE0 + QK_ROPE].set(sin32)
    return cos_t, sin_t


def _swap_perm():
    n_freq = QK_ROPE // 4
    r = np.arange(QK_ROPE)
    axis, half, f = r // (2 * n_freq), (r // n_freq) % 2, r % n_freq
    return axis * 2 * n_freq + (1 - half) * n_freq + f


def _pack_weights(w_in, w_uq, w_ukv, w_s, b_s, w_r, b_r, w_i, b_i, lru_lambda):
    depth = w_in.shape[0]
    perm = _swap_perm()
    dq = QK_NOPE + QK_ROPE
    wq = w_uq.reshape(depth, Q_RANK, HEADS, dq)
    zq = jnp.zeros((depth, Q_RANK, HEADS, HEAD_PAD - dq), w_uq.dtype)
    main = jnp.concatenate([wq, zq], axis=-1)
    swp = jnp.concatenate([jnp.zeros_like(wq[..., :QK_NOPE]), wq[..., QK_NOPE + perm], zq], axis=-1)
    wuq = jnp.concatenate([main.reshape(depth, Q_RANK, -1), swp.reshape(depth, Q_RANK, -1)], axis=-1)
    wkv = w_ukv.reshape(depth, KV_RANK, HEADS, QK_NOPE + V_DIM)
    wuk = jnp.concatenate([wkv[..., :QK_NOPE], jnp.zeros((depth, KV_RANK, HEADS, HEAD_PAD - QK_NOPE), w_ukv.dtype)],
                          axis=-1).reshape(depth, KV_RANK, -1)
    wuv = wkv[..., QK_NOPE:].reshape(depth, KV_RANK, -1)
    wkr = w_in[:, :, OFF_KR:OFF_KR + QK_ROPE]
    zl = jnp.zeros((depth, D, ROPE_LANE0), w_in.dtype)
    zr = jnp.zeros((depth, D, HEAD_PAD - ROPE_LANE0 - QK_ROPE), w_in.dtype)
    wb = jnp.concatenate([w_in[:, :, OFF_CQ:OFF_KR], zl, wkr, zr, zl, wkr[:, :, perm], zr], axis=-1)
    eye = jnp.eye(C_BLOCKS, dtype=w_r.dtype)

    def dense(wblk):
        full = jnp.einsum("ldhij,hk->ldhikj", wblk, eye)
        return full.reshape(depth, 2, C_WIDTH, C_WIDTH)

    wri = jnp.concatenate([dense(w_r), dense(w_i)], axis=-1).reshape(depth * 2, C_WIDTH, 2 * C_WIDTH)
    bri = jnp.concatenate([b_r, b_i], axis=-1).reshape(depth * 2, 1, 2 * C_WIDTH)
    gw = A_WIDTH // A_GROUPS
    bs_full = jnp.broadcast_to(jnp.swapaxes(b_s, 1, 2)[:, :, :, None],
                               (depth, CHUNK, A_GROUPS, gw)).reshape(depth, CHUNK, A_WIDTH)
    return {
        "wa": w_in[:, :, OFF_AU:OFF_CQ].astype(BF16),
        "wb": wb.astype(BF16),
        "wc": w_in[:, :, OFF_CX:OFF_GATE].astype(BF16),
        "wg": w_in[:, :, OFF_GATE:].astype(BF16),
        "wuq": wuq.astype(BF16), "wuk": wuk.astype(BF16), "wuv": wuv.astype(BF16),
        "w_s": w_s.astype(BF16), "bs_full": bs_full,
        "wri": wri.astype(BF16), "bri": bri,
        "lam": lru_lambda.reshape(depth * 2, 1, C_WIDTH),
    }


def kernel(x, c, ctx, c_ctx, w_ada, b_ada, w_in, a_ln_g, a_ln_b, w_s, b_s, q_norm_g, w_uq,
           kv_norm_g, w_ukv, conv_w, conv_b, w_r, b_r, w_i, b_i, lru_lambda, w_br, w_o,
           ln1_g, ln1_b, w_router, router_bias, w_gate, w_up, w_down, ln2_g, ln2_b):
    depth = w_in.shape[0]
    bsz, seq, _ = x.shape
    ctx_len = ctx.shape[1]
    assert ctx_len == TILE and seq % TILE == 0 and bsz == SUBLANES
    t = ctx_len + seq
    nct = t // TILE
    n = bsz * t
    assert n % MM_TILE == 0 and (2 * n) % MOE_TILE == 0
    alpha = (2.0 * depth) ** 0.25

    row3 = lambda a: a.reshape(depth, 1, a.shape[-1])
    w = _pack_weights(w_in, w_uq, w_ukv, w_s, b_s, w_r, b_r, w_i, b_i, lru_lambda)
    w.update({
        "a_ln_g": row3(a_ln_g), "a_ln_b": row3(a_ln_b),
        "q_norm_g": row3(q_norm_g), "kv_norm_g": row3(kv_norm_g),
        "conv_w": conv_w, "conv_b": row3(conv_b),
        "w_br": w_br.astype(BF16), "w_o": w_o.astype(BF16),
        "ln1_g": row3(ln1_g), "ln1_b": row3(ln1_b), "ln2_g": row3(ln2_g), "ln2_b": row3(ln2_b),
        "w_gate": w_gate.astype(BF16), "w_up": w_up.astype(BF16), "w_down": w_down.astype(BF16),
    })
    cos_t, sin_t = _rope_tables(seq, ctx_len)
    wrt = w_router.T
    rbias = router_bias.reshape(N_EXPERTS, 1)

    cond_rows = jnp.zeros((2 * SUBLANES, D), F32).at[:bsz].set(c).at[bsz].set(c_ctx)
    mod = _mod_call(cond_rows, w_ada, b_ada)
    tile_row = np.array([bsz if j == 0 else b for b in range(bsz) for j in range(nct)], np.int32)
    mod_tiles = mod[:, tile_row, :].reshape(depth, n // TILE, 1, 6 * D)

    xs = jnp.concatenate([ctx, x], axis=1).reshape(n, D)
    for l in range(depth):
        mt = mod_tiles[l]
        h, ya, q, k, v, xc, cg = _proj_call(l, xs, mt, cos_t, sin_t, w, nct)
        yb = _attn_call(q, k, v, bsz, nct)
        rec_tm = _scan_call(l, jnp.swapaxes(xc.reshape(bsz, t, C_WIDTH), 0, 1), w, nct)
        rec = jnp.swapaxes(rec_tm, 1, 2).reshape(2, n, C_WIDTH)
        x1, f, re, rw, rr = _merge_call(l, xs, h, ya, yb, cg, rec, mt, w, wrt, rbias, alpha)
        pos, src_tok, tile_expert, nvalid = _dispatch_plan(re, rr, n // TILE)
        ys = _expert_call(l, f, src_tok, tile_expert, nvalid, w)
        xs = _combine_call(l, pos, ys, rw.T, x1, mt, w, alpha)
    return xs.reshape(bsz, t, D)[:, ctx_len:, :]
```

```python
import functools

import numpy as np
import jax
import jax.numpy as jnp
from jax import lax
from jax.experimental import pallas as pl
from jax.experimental.pallas import tpu as pltpu

D = 1024
GRID_W = 64
A_WIDTH = 512
A_GROUPS = 4
CHUNK = 128
HEADS = 8
QK_NOPE = 64
QK_ROPE = 32
V_DIM = 64
Q_RANK = 256
KV_RANK = 128
ROPE_BASE = 10000.0
C_WIDTH = 512
C_BLOCKS = 8
C_BLOCK_W = C_WIDTH // C_BLOCKS
CONV_W = 4
CONV_LEFT = 2
LRU_C = 8.0
N_BRANCH = 3
BRANCH_W = 512
OFF_AU = 0
OFF_AV = OFF_AU + A_WIDTH
OFF_CQ = OFF_AV + A_WIDTH
OFF_CKV = OFF_CQ + Q_RANK
OFF_KR = OFF_CKV + KV_RANK
OFF_CX = OFF_KR + QK_ROPE
OFF_CG = OFF_CX + C_WIDTH
OFF_GATE = OFF_CG + C_WIDTH
N_EXPERTS = 16
N_GROUPS = 4
EXPERTS_PER_GROUP = N_EXPERTS // N_GROUPS
D_EXPERT = 512
ROUTED_SCALE = 2.5
EPS = 1e-6

LANES = 128
SUBLANES = 8
TILE = 256
HEAD_PAD = 128
ROPE_LANE0 = QK_NOPE
MOE_TILE = 256
ATT_Q = 128
ATT_K = 2 * LANES
MM_TILE = 512
VMEM_LIMIT = 56 * 1024 * 1024

BF16 = jnp.bfloat16
F32 = jnp.float32


def _cparams(sem):
    return pltpu.CompilerParams(dimension_semantics=sem, vmem_limit_bytes=VMEM_LIMIT)


def _dot(a, b):
    return jnp.dot(a, b, preferred_element_type=F32)


def _sigmoid(v):
    return 0.5 * jnp.tanh(0.5 * v) + 0.5


def _layer_norm(v, g, b):
    mu = jnp.mean(v, axis=-1, keepdims=True)
    c = v - mu
    var = jnp.mean(c * c, axis=-1, keepdims=True)
    return (c * lax.rsqrt(var + EPS)) * g + b


def _rms_norm(v, g):
    ms = jnp.mean(v * v, axis=-1, keepdims=True)
    return (v * lax.rsqrt(ms + EPS)) * g


def _mod_kernel(c_ref, w_ref, b_ref, o_ref):
    c = c_ref[...]
    cond = (c * jax.nn.sigmoid(c)).astype(BF16)
    o_ref[0] = _dot(cond, w_ref[0].astype(BF16)) + b_ref[0]


def _mod_call(cond_rows, w_ada, b_ada):
    depth = w_ada.shape[0]
    rows = cond_rows.shape[0]
    ncol = 6 * D
    bn = 1536
    return pl.pallas_call(
        _mod_kernel,
        grid=(depth, ncol // bn),
        in_specs=[
            pl.BlockSpec((rows, D), lambda l, j: (0, 0)),
            pl.BlockSpec((1, D, bn), lambda l, j: (l, 0, j)),
            pl.BlockSpec((1, 1, bn), lambda l, j: (l, 0, j)),
        ],
        out_specs=pl.BlockSpec((1, rows, bn), lambda l, j: (l, 0, j)),
        out_shape=jax.ShapeDtypeStruct((depth, rows, ncol), F32),
        compiler_params=_cparams(("parallel", "parallel")),
        name="mod",
    )(cond_rows, w_ada, b_ada.reshape(depth, 1, ncol))


def _proj_kernel(x_ref, mod_ref, cos0_ref, cos1_ref, sin0_ref, sin1_ref, wa_ref, wb_ref, wc_ref,
                 alng_ref, alnb_ref, ws_ref, bs_ref, qg_ref, wuq_ref, kvg_ref,
                 wuk_ref, wuv_ref,
                 h_ref, ya_ref, q_ref, k_ref, v_ref, xc_ref, cg_ref):
    halves = []
    for sub in range(MM_TILE // TILE):
        rows = slice(sub * TILE, (sub + 1) * TILE)
        m = mod_ref[sub]
        halves.append((x_ref[rows, :] * (1.0 + m[:, D:2 * D]) + m[:, 0:D]).astype(BF16))
    h = jnp.concatenate(halves, axis=0)
    h_ref[...] = h

    uv = _dot(h, wa_ref[0])
    u = jax.nn.gelu(uv[:, :A_WIDTH])
    v = _layer_norm(jax.nn.gelu(uv[:, A_WIDTH:]), alng_ref[0], alnb_ref[0]).astype(BF16)
    gw = A_WIDTH // A_GROUPS
    for c in range(MM_TILE // CHUNK):
        rows = slice(c * CHUNK, (c + 1) * CHUNK)
        for g in range(A_GROUPS):
            cols = slice(g * gw, (g + 1) * gw)
            mixed = _dot(ws_ref[0, g], v[rows, cols]) + bs_ref[0, :, cols]
            ya_ref[rows, cols] = (u[rows, cols] * mixed).astype(BF16)

    pb = _dot(h, wb_ref[0])
    cos_t = jnp.concatenate([cos0_ref[...], cos1_ref[...]], axis=0)
    sin_t = jnp.concatenate([sin0_ref[...], sin1_ref[...]], axis=0)
    cqn = _rms_norm(pb[:, :Q_RANK], qg_ref[0]).astype(BF16)
    qq = _dot(cqn, wuq_ref[0])
    hw = HEADS * HEAD_PAD
    for hd in range(HEADS):
        a = slice(hd * HEAD_PAD, (hd + 1) * HEAD_PAD)
        b = slice(hw + hd * HEAD_PAD, hw + (hd + 1) * HEAD_PAD)
        q_ref[:, a] = (qq[:, a] * cos_t + qq[:, b] * sin_t).astype(BF16)
    ckvn = _rms_norm(pb[:, Q_RANK:Q_RANK + KV_RANK], kvg_ref[0]).astype(BF16)
    o_kr = Q_RANK + KV_RANK
    kr = pb[:, o_kr:o_kr + HEAD_PAD] * cos_t + pb[:, o_kr + HEAD_PAD:o_kr + 2 * HEAD_PAD] * sin_t
    kk = _dot(ckvn, wuk_ref[0])
    for hd in range(HEADS):
        a = slice(hd * HEAD_PAD, (hd + 1) * HEAD_PAD)
        k_ref[:, a] = (kk[:, a] + kr).astype(BF16)
    v_ref[...] = _dot(ckvn, wuv_ref[0]).astype(BF16)

    pc = _dot(h, wc_ref[0])
    xc_ref[...] = pc[:, :C_WIDTH]
    cg_ref[...] = jax.nn.gelu(pc[:, C_WIDTH:])


def _const_spec(a, l):
    nz = (0,) * (a.ndim - 1)
    return pl.BlockSpec((1,) + a.shape[1:], lambda i: (l,) + nz, pipeline_mode=pl.Buffered(1))


def _proj_call(l, x, mod_tiles, cos_t, sin_t, w, nct):
    n = x.shape[0]
    nsub = MM_TILE // TILE
    assert nsub == 2
    tok = lambda width: pl.BlockSpec((MM_TILE, width), lambda i: (i, 0))
    pos0 = pl.BlockSpec((TILE, HEAD_PAD), lambda i: ((nsub * i) % nct, 0))
    pos1 = pl.BlockSpec((TILE, HEAD_PAD), lambda i: ((nsub * i + 1) % nct, 0))
    outs = [
        (D, BF16), (A_WIDTH, BF16), (HEADS * HEAD_PAD, BF16), (HEADS * HEAD_PAD, BF16),
        (HEADS * V_DIM, BF16), (C_WIDTH, F32), (C_WIDTH, F32),
    ]
    names = ["wa", "wb", "wc", "a_ln_g", "a_ln_b", "w_s", "bs_full", "q_norm_g", "wuq",
             "kv_norm_g", "wuk", "wuv"]
    return pl.pallas_call(
        _proj_kernel,
        grid=(n // MM_TILE,),
        in_specs=[
            tok(D),
            pl.BlockSpec((nsub, 1, 6 * D), lambda i: (i, 0, 0)),
            pos0, pos1, pos0, pos1,
        ] + [_const_spec(w[nm], l) for nm in names],
        out_specs=[tok(wd) for wd, _ in outs],
        out_shape=[jax.ShapeDtypeStruct((n, wd), dt) for wd, dt in outs],
        compiler_params=_cparams(("parallel",)),
        name="proj",
    )(x, mod_tiles, cos_t, cos_t, sin_t, sin_t, *[w[nm] for nm in names])


def _attn_kernel(q_ref, k_ref, v_ref, o_ref, s_buf, vext):
    j = pl.program_id(2)
    c2 = float(QK_NOPE + QK_ROPE) ** -0.5 * float(np.log2(np.e))
    nt_dims = (((1,), (1,)), ((), ()))
    nv = 2 * V_DIM

    @pl.when(j == 0)
    def _():
        vext[:, :nv] = v_ref[...]
        vext[:, nv:] = jnp.ones((vext.shape[0], vext.shape[1] - nv), BF16)

    def attend(nk):
        for qh in range(TILE // ATT_Q):
            rows = slice(qh * ATT_Q, (qh + 1) * ATT_Q)
            for hd in range(2):
                cols = slice(hd * HEAD_PAD, (hd + 1) * HEAD_PAD)
                q = q_ref[rows, cols]
                m = None
                for c in range(nk // ATT_K):
                    keys = slice(c * ATT_K, (c + 1) * ATT_K)
                    s = lax.dot_general(q, k_ref[keys, cols], nt_dims, preferred_element_type=F32)
                    s_buf[:, keys] = s
                    sm = jnp.maximum(s[:, :LANES], s[:, LANES:])
                    m = sm if m is None else jnp.maximum(m, sm)
                mx = jnp.max(m, axis=-1, keepdims=True)
                acc = None
                for c in range(nk // ATT_K):
                    keys = slice(c * ATT_K, (c + 1) * ATT_K)
                    p = jnp.exp2((s_buf[:, keys] - mx) * c2)
                    pv = _dot(p.astype(BF16), vext[keys, :])
                    acc = pv if acc is None else acc + pv
                out = (acc[:, :nv] / acc[:, nv:]).astype(BF16)
                vcols = slice(hd * V_DIM, (hd + 1) * V_DIM)
                o_ref[rows, vcols] = out[:, vcols]

    @pl.when(j == 0)
    def _():
        attend(TILE)

    @pl.when(j > 0)
    def _():
        attend(k_ref.shape[0])


def _attn_call(q, k, v, bsz, nct):
    n = q.shape[0]
    t = nct * TILE
    npair = HEADS // 2
    return pl.pallas_call(
        _attn_kernel,
        grid=(bsz, npair, nct),
        in_specs=[
            pl.BlockSpec((TILE, 2 * HEAD_PAD), lambda b, p, j: (b * nct + j, p)),
            pl.BlockSpec((t, 2 * HEAD_PAD), lambda b, p, j: (b, p)),
            pl.BlockSpec((t, 2 * V_DIM), lambda b, p, j: (b, p)),
        ],
        out_specs=pl.BlockSpec((TILE, 2 * V_DIM), lambda b, p, j: (b * nct + j, p)),
        out_shape=jax.ShapeDtypeStruct((n, HEADS * V_DIM), BF16),
        scratch_shapes=[pltpu.VMEM((ATT_Q, t), F32), pltpu.VMEM((t, 4 * V_DIM), BF16)],
        compiler_params=_cparams(("parallel", "parallel", "arbitrary")),
        name="attn",
    )(q, k, v)


def _scan_chunk(d, s, nct):
    return jnp.where(d == 0, s, jnp.where(s == 0, 0, nct - s))


def _scan_kernel(xc_ref, prev_ref, next_ref, cw_ref, cb_ref, wri_ref, bri_ref, lam_ref,
                 o_ref, a_s, bx_s, carry, *, nct):
    d = pl.program_id(0)
    s = pl.program_id(1)
    chunk = _scan_chunk(d, s, nct)
    bsz = xc_ref.shape[1]
    has_prev = jnp.logical_and(chunk != 0, chunk != 1)
    has_next = jnp.logical_and(chunk != 0, chunk != nct - 1)
    cur = xc_ref[...]
    left = CONV_LEFT
    right = CONV_W - 1 - CONV_LEFT
    xp = jnp.concatenate([jnp.where(has_prev, prev_ref[...], 0.0), cur,
                          jnp.where(has_next, next_ref[...], 0.0)], axis=0)
    xconv = cb_ref[...]
    for kk in range(CONV_W):
        xconv = xconv + xp[kk:kk + TILE] * cw_ref[0, kk:kk + 1, :]
    assert xp.shape[0] == TILE + left + right
    x2 = xconv.reshape(TILE * bsz, C_WIDTH)
    ri = _dot(x2.astype(BF16), wri_ref[0]) + bri_ref[0]
    r = _sigmoid(ri[:, :C_WIDTH])
    ig = _sigmoid(ri[:, C_WIDTH:])
    lam = lam_ref[0]
    log_sig = jnp.minimum(lam, 0.0) - jnp.log1p(jnp.exp(-jnp.abs(lam)))
    log_a = LRU_C * r * log_sig
    a = jnp.exp(log_a)
    a_s[...] = a
    bx_s[...] = jnp.sqrt(-jnp.tanh(log_a) * (a * a + 1.0)) * (ig * x2)

    @pl.when(s == 0)
    def _():
        carry[...] = jnp.zeros_like(carry)

    def step(t, hprev):
        tt = jnp.where(d == 0, t, TILE - 1 - t)
        rows = pl.ds(pl.multiple_of(tt * bsz, bsz), bsz)
        hnew = a_s[rows, :] * hprev + bx_s[rows, :]
        o_ref[0, tt] = hnew
        return hnew

    carry[...] = lax.fori_loop(0, TILE, step, carry[...], unroll=8)


def _scan_call(l, xc3, w, nct):
    t, bsz, _ = xc3.shape
    left = CONV_LEFT
    right = CONV_W - 1 - CONV_LEFT
    assert TILE % left == 0 and right == 1
    cur = lambda d, s: (_scan_chunk(d, s, nct), 0, 0)
    prv = lambda d, s: (jnp.maximum(_scan_chunk(d, s, nct) * (TILE // left) - 1, 0), 0, 0)
    nxt = lambda d, s: (jnp.minimum((_scan_chunk(d, s, nct) + 1) * TILE, t - 1), 0, 0)
    return pl.pallas_call(
        functools.partial(_scan_kernel, nct=nct),
        grid=(2, nct),
        in_specs=[
            pl.BlockSpec((TILE, bsz, C_WIDTH), cur),
            pl.BlockSpec((left, bsz, C_WIDTH), prv),
            pl.BlockSpec((right, bsz, C_WIDTH), nxt),
            pl.BlockSpec((1, CONV_W, C_WIDTH), lambda d, s: (l, 0, 0)),
            pl.BlockSpec((1, 1, C_WIDTH), lambda d, s: (l, 0, 0)),
            pl.BlockSpec((1, C_WIDTH, 2 * C_WIDTH), lambda d, s: (2 * l + d, 0, 0)),
            pl.BlockSpec((1, 1, 2 * C_WIDTH), lambda d, s: (2 * l + d, 0, 0)),
            pl.BlockSpec((1, 1, C_WIDTH), lambda d, s: (2 * l + d, 0, 0)),
        ],
        out_specs=pl.BlockSpec((1, TILE, bsz, C_WIDTH),
                               lambda d, s: (d, _scan_chunk(d, s, nct), 0, 0)),
        out_shape=jax.ShapeDtypeStruct((2, t, bsz, C_WIDTH), F32),
        scratch_shapes=[
            pltpu.VMEM((TILE * bsz, C_WIDTH), F32),
            pltpu.VMEM((TILE * bsz, C_WIDTH), F32),
            pltpu.VMEM((bsz, C_WIDTH), F32),
        ],
        compiler_params=_cparams(("arbitrary", "arbitrary")),
        name="scan",
    )(xc3, xc3, xc3, w["conv_w"], w["conv_b"], w["wri"], w["bri"], w["lam"])


def _top2_sum(v):
    hi1, lo1 = jnp.maximum(v[0], v[1]), jnp.minimum(v[0], v[1])
    hi2, lo2 = jnp.maximum(v[2], v[3]), jnp.minimum(v[2], v[3])
    top = jnp.maximum(hi1, hi2)
    second = jnp.maximum(jnp.minimum(hi1, hi2), jnp.maximum(lo1, lo2))
    return top + second


def _route(scores, sel):
    gs = [_top2_sum(sel[g * EXPERTS_PER_GROUP:(g + 1) * EXPERTS_PER_GROUP]) for g in range(N_GROUPS)]
    best = jnp.zeros_like(gs[0], dtype=jnp.int32)
    best_v = gs[0]
    for g in range(1, N_GROUPS):
        upd = gs[g] > best_v
        best = jnp.where(upd, g, best)
        best_v = jnp.where(upd, gs[g], best_v)

    def pick(rows, k):
        out = rows[k]
        for g in range(1, N_GROUPS):
            out = jnp.where(best == g, rows[g * EXPERTS_PER_GROUP + k], out)
        return out

    in_sel = [pick(sel, k) for k in range(EXPERTS_PER_GROUP)]
    in_sc = [pick(scores, k) for k in range(EXPERTS_PER_GROUP)]
    neg = jnp.full_like(in_sel[0], -jnp.inf)

    def argmax4(vals):
        idx = jnp.zeros_like(best)
        val = vals[0]
        for k in range(1, EXPERTS_PER_GROUP):
            upd = vals[k] > val
            idx = jnp.where(upd, k, idx)
            val = jnp.where(upd, vals[k], val)
        return idx

    i1 = argmax4(in_sel)
    i2 = argmax4([jnp.where(i1 == k, neg, in_sel[k]) for k in range(EXPERTS_PER_GROUP)])

    def take(vals, idx):
        out = vals[0]
        for k in range(1, EXPERTS_PER_GROUP):
            out = jnp.where(idx == k, vals[k], out)
        return out

    w1 = take(in_sc, i1)
    w2 = take(in_sc, i2)
    den = w1 + w2
    return (best * EXPERTS_PER_GROUP + i1, best * EXPERTS_PER_GROUP + i2,
            ROUTED_SCALE * w1 / den, ROUTED_SCALE * w2 / den)


def _merge_kernel(x_ref, h_ref, ya_ref, yb_ref, cg_ref, rf_ref, rb_ref, mod_ref,
                  wg_ref, wbr_ref, wo_ref, lng_ref, lnb_ref, wrt_ref, rbias_ref,
                  x1_ref, f_ref, re_ref, rw_ref, rr_ref, *, alpha):
    h = h_ref[...]
    yc = (cg_ref[...] * (rf_ref[0] + rb_ref[0])).astype(BF16)
    ys = (ya_ref[...], yb_ref[...], yc)
    mix = None
    for kb in range(N_BRANCH):
        gate = _sigmoid(_dot(h, wg_ref[0, :, kb * D:(kb + 1) * D]))
        term = gate * _dot(ys[kb], wbr_ref[0, kb])
        mix = term if mix is None else mix + term
    y = _dot(mix.astype(BF16), wo_ref[0])

    row_i = lax.broadcasted_iota(jnp.int32, (TILE, TILE), 0)
    col_i = lax.broadcasted_iota(jnp.int32, (TILE, TILE), 1)
    tri = jnp.where(row_i <= col_i, 1.0, 0.0).astype(BF16)
    for sub in range(MM_TILE // TILE):
        rows = slice(sub * TILE, (sub + 1) * TILE)
        m = mod_ref[sub]
        g1 = m[:, 2 * D:3 * D]
        sh2 = m[:, 3 * D:4 * D]
        sc2 = m[:, 4 * D:5 * D]
        x1 = _layer_norm(alpha * x_ref[rows, :] + g1 * y[rows, :], lng_ref[0], lnb_ref[0])
        x1_ref[rows, :] = x1
        f = x1 * (1.0 + sc2) + sh2
        f_ref[rows, :] = f

        logits = lax.dot_general(wrt_ref[...], f, (((1,), (1,)), ((), ())),
                                 precision=lax.Precision.HIGHEST, preferred_element_type=F32)
        sc = jax.nn.sigmoid(logits)
        sl = sc + rbias_ref[...]
        scores = [sc[e:e + 1, :] for e in range(N_EXPERTS)]
        sel = [sl[e:e + 1, :] for e in range(N_EXPERTS)]
        e1, e2, w1, w2 = _route(scores, sel)
        re_ref[:, rows] = jnp.concatenate([e1, e2], axis=0)
        rw_ref[:, rows] = jnp.concatenate([w1, w2], axis=0)
        ranks = []
        for ek in (e1, e2):
            onehot = jnp.concatenate([jnp.where(ek == e, 1.0, 0.0) for e in range(N_EXPERTS)], axis=0)
            prefix = _dot(onehot.astype(BF16), tri)
            ranks.append(jnp.sum(onehot * prefix, axis=0, keepdims=True) - 1.0)
        rr_ref[:, rows] = jnp.concatenate(ranks, axis=0).astype(jnp.int32)


def _merge_call(l, x, h, ya, yb, cg, rec, mod_tiles, w, wrt, rbias, alpha):
    n = x.shape[0]
    tok = lambda width: pl.BlockSpec((MM_TILE, width), lambda i: (i, 0))
    names = ["wg", "w_br", "w_o", "ln1_g", "ln1_b"]
    return pl.pallas_call(
        functools.partial(_merge_kernel, alpha=alpha),
        grid=(n // MM_TILE,),
        in_specs=[
            tok(D), tok(D), tok(A_WIDTH), tok(HEADS * V_DIM), tok(C_WIDTH),
            pl.BlockSpec((1, MM_TILE, C_WIDTH), lambda i: (0, i, 0)),
            pl.BlockSpec((1, MM_TILE, C_WIDTH), lambda i: (1, i, 0)),
            pl.BlockSpec((MM_TILE // TILE, 1, 6 * D), lambda i: (i, 0, 0)),
        ] + [_const_spec(w[nm], l) for nm in names] + [
            pl.BlockSpec(wrt.shape, lambda i: (0, 0)),
            pl.BlockSpec(rbias.shape, lambda i: (0, 0)),
        ],
        out_specs=[tok(D), tok(D)] + [pl.BlockSpec((2, MM_TILE), lambda i: (0, i))] * 3,
        out_shape=[jax.ShapeDtypeStruct((n, D), F32), jax.ShapeDtypeStruct((n, D), F32),
                   jax.ShapeDtypeStruct((2, n), jnp.int32), jax.ShapeDtypeStruct((2, n), F32),
                   jax.ShapeDtypeStruct((2, n), jnp.int32)],
        compiler_params=_cparams(("parallel",)),
        name="merge",
    )(x, h, ya, yb, cg, rec, rec, mod_tiles, w["wg"], w["w_br"], w["w_o"], w["ln1_g"], w["ln1_b"],
      wrt, rbias)


def _row_gather_copy(src_hbm, row, dst, dst_row, sem):
    return pltpu.make_async_copy(src_hbm.at[pl.ds(row, 1), :], dst.at[pl.ds(dst_row, 1), :], sem)


def _tile_wait(src_hbm, dst, sem):
    pltpu.make_async_copy(src_hbm.at[pl.ds(0, dst.shape[0]), :], dst, sem).wait()


def _expert_kernel(src_ref, texp_ref, nvalid_ref, f_hbm, wgt_ref, wup_ref, wdn_ref,
                   o_ref, buf, sem):
    i = pl.program_id(0)
    nvalid = nvalid_ref[0]
    slot = i % 2

    def issue(tile, sl):
        base = tile * MOE_TILE
        for r in range(MOE_TILE):
            _row_gather_copy(f_hbm, src_ref[base + r], buf.at[sl], r, sem.at[sl]).start()

    @pl.when(jnp.logical_and(i == 0, nvalid > 0))
    def _():
        issue(0, 0)

    @pl.when(i + 1 < nvalid)
    def _():
        issue(i + 1, 1 - slot)

    @pl.when(i < nvalid)
    def _():
        _tile_wait(f_hbm, buf.at[slot], sem.at[slot])
        xin = buf[slot].astype(BF16)
        gt = _dot(xin, wgt_ref[0, 0])
        t = (gt * _sigmoid(gt) * _dot(xin, wup_ref[0, 0])).astype(BF16)
        o_ref[...] = _dot(t, wdn_ref[0, 0])

    @pl.when(i >= nvalid)
    def _():
        o_ref[...] = jnp.zeros_like(o_ref)


def _expert_call(l, f, src_tok, tile_expert, nvalid, w):
    ntile = tile_expert.shape[0]
    wspec = lambda a, b: pl.BlockSpec((1, 1, a, b), lambda i, src, texp, nv: (l, texp[i], 0, 0))
    return pl.pallas_call(
        _expert_kernel,
        grid_spec=pltpu.PrefetchScalarGridSpec(
            num_scalar_prefetch=3,
            grid=(ntile,),
            in_specs=[pl.BlockSpec(memory_space=pl.ANY),
                      wspec(D, D_EXPERT), wspec(D, D_EXPERT), wspec(D_EXPERT, D)],
            out_specs=pl.BlockSpec((MOE_TILE, D), lambda i, src, texp, nv: (i, 0)),
            scratch_shapes=[pltpu.VMEM((2, MOE_TILE, D), F32), pltpu.SemaphoreType.DMA((2,))],
        ),
        out_shape=jax.ShapeDtypeStruct((ntile * MOE_TILE, D), F32),
        compiler_params=_cparams(("arbitrary",)),
        name="expert",
    )(src_tok, tile_expert, nvalid, f, w["w_gate"], w["w_up"], w["w_down"])


def _combine_kernel(pos_ref, ys_hbm, rw_ref, x1_ref, mod_ref, lng_ref, lnb_ref, o_ref, buf, sem,
                    *, alpha, n):
    i = pl.program_id(0)
    slot = i % 2

    def issue(tile, sl):
        for k in range(2):
            base = k * n + tile * TILE
            for r in range(TILE):
                _row_gather_copy(ys_hbm, pos_ref[base + r], buf.at[sl, k], r, sem.at[sl]).start(
                    priority=r % 2)

    @pl.when(i == 0)
    def _():
        issue(0, 0)

    @pl.when(i + 1 < pl.num_programs(0))
    def _():
        issue(i + 1, 1 - slot)

    for k in range(2):
        _tile_wait(ys_hbm, buf.at[slot, k], sem.at[slot])
    rw = rw_ref[...]
    fo = rw[:, 0:1] * buf[slot, 0] + rw[:, 1:2] * buf[slot, 1]
    g2 = mod_ref[0, :, 5 * D:6 * D]
    o_ref[...] = _layer_norm(alpha * x1_ref[...] + g2 * fo, lng_ref[0], lnb_ref[0])


def _combine_call(l, pos, ys, rw_t, x1, mod_tiles, w, alpha):
    n = x1.shape[0]
    nt = n // TILE
    tok = lambda width: pl.BlockSpec((TILE, width), lambda i, p: (i, 0))
    lay3 = lambda a: pl.BlockSpec((1,) + a.shape[1:], lambda i, p: (l, 0, 0))
    return pl.pallas_call(
        functools.partial(_combine_kernel, alpha=alpha, n=n),
        grid_spec=pltpu.PrefetchScalarGridSpec(
            num_scalar_prefetch=1,
            grid=(nt,),
            in_specs=[pl.BlockSpec(memory_space=pl.ANY), tok(2), tok(D),
                      pl.BlockSpec((1, 1, 6 * D), lambda i, p: (i, 0, 0)),
                      lay3(w["ln2_g"]), lay3(w["ln2_b"])],
            out_specs=tok(D),
            scratch_shapes=[pltpu.VMEM((2, 2, TILE, D), F32), pltpu.SemaphoreType.DMA((2,))],
        ),
        out_shape=jax.ShapeDtypeStruct((n, D), F32),
        compiler_params=_cparams(("arbitrary",)),
        name="combine",
    )(pos, ys, rw_t, x1, mod_tiles, w["ln2_g"], w["ln2_b"])


def _invert_kernel(pos_ref, pad_lo_ref, pad_hi_ref, src_ref):
    ntok = pos_ref.shape[0] // 2
    for e in range(pad_lo_ref.shape[0]):
        def clear(i, c):
            src_ref[i] = 0
            return c

        lax.fori_loop(pad_lo_ref[e], pad_hi_ref[e], clear, 0)
    for k in range(2):
        def place(t, c, k=k):
            src_ref[pos_ref[k * ntok + t]] = t
            return c

        lax.fori_loop(0, ntok, place, 0, unroll=16)


def _invert_call(pos, pad_lo, pad_hi, nslots):
    smem = pl.BlockSpec(memory_space=pltpu.SMEM)
    return pl.pallas_call(
        _invert_kernel,
        in_specs=[smem, smem, smem],
        out_specs=smem,
        out_shape=jax.ShapeDtypeStruct((nslots,), jnp.int32),
        name="invert",
    )(pos, pad_lo, pad_hi)


def _dispatch_plan(re, rr, nt):
    n = re.shape[1]
    onehot = re.reshape(2, nt, TILE, 1) == jnp.arange(N_EXPERTS, dtype=jnp.int32)
    counts = jnp.sum(onehot, axis=2, dtype=jnp.int32)
    flat = jnp.swapaxes(counts, 0, 1).reshape(nt * 2, N_EXPERTS)
    incl = jnp.cumsum(flat, axis=0)
    total = incl[-1]
    off = jnp.swapaxes((incl - flat).reshape(nt, 2, N_EXPERTS), 0, 1)
    padded = (total + MOE_TILE - 1) // MOE_TILE * MOE_TILE
    ends = jnp.cumsum(padded)
    base = ends - padded
    start = (base + off)[:, :, None, :]
    pos = jnp.sum(jnp.where(onehot, start, 0), axis=-1).reshape(2, n) + rr
    ntile = (2 * n) // MOE_TILE + N_EXPERTS
    tile_start = jnp.arange(ntile, dtype=jnp.int32) * MOE_TILE
    tile_expert = jnp.minimum(jnp.sum(ends[None, :] <= tile_start[:, None], axis=1), N_EXPERTS - 1)
    nvalid = (ends[-1] // MOE_TILE).reshape(1)
    pos = pos.reshape(-1).astype(jnp.int32)
    nslots = ntile * MOE_TILE
    pad_lo = jnp.concatenate([base + total, ends[-1:]]).astype(jnp.int32)
    pad_hi = jnp.concatenate([ends, jnp.full((1,), nslots, ends.dtype)]).astype(jnp.int32)
    src_tok = _invert_call(pos, pad_lo, pad_hi, nslots)
    return pos, src_tok, tile_expert.astype(jnp.int32), nvalid.astype(jnp.int32)


def _rope_tables(seq, ctx_len):
    rows = seq // GRID_W
    row = jnp.repeat(jnp.arange(rows, dtype=F32), GRID_W)
    col = jnp.tile(jnp.arange(GRID_W, dtype=F32), rows)
    n_freq = QK_ROPE // 4
    inv = ROPE_BASE ** (-jnp.arange(n_freq, dtype=F32) / n_freq)
    ang = jnp.stack([row[:, None] * inv, col[:, None] * inv], axis=1)
    cos, sin = jnp.cos(ang), jnp.sin(ang)
    cos32 = jnp.broadcast_to(cos[:, :, None, :], (seq, 2, 2, n_freq)).reshape(seq, QK_ROPE)
    sign = jnp.array([-1.0, 1.0], F32)[None, None, :, None]
    sin32 = (jnp.broadcast_to(sin[:, :, None, :], (seq, 2, 2, n_freq)) * sign).reshape(seq, QK_ROPE)
    t = ctx_len + seq
    cos_t = jnp.ones((t, HEAD_PAD), F32).at[ctx_len:, ROPE_LANE0:ROPE_LANE0 + QK_ROPE].set(cos32)
    sin_t = jnp.zeros((t, HEAD_PAD), F32).at[ctx_len:, ROPE_LANE0:ROPE_LANE0 + QK_ROPE].set(sin32)
    return cos_t, sin_t


def _swap_perm():
    n_freq = QK_ROPE // 4
    r = np.arange(QK_ROPE)
    axis, half, f = r // (2 * n_freq), (r // n_freq) % 2, r % n_freq
    return axis * 2 * n_freq + (1 - half) * n_freq + f


def _pack_weights(w_in, w_uq, w_ukv, w_s, b_s, w_r, b_r, w_i, b_i, lru_lambda):
    depth = w_in.shape[0]
    perm = _swap_perm()
    dq = QK_NOPE + QK_ROPE
    wq = w_uq.reshape(depth, Q_RANK, HEADS, dq)
    zq = jnp.zeros((depth, Q_RANK, HEADS, HEAD_PAD - dq), w_uq.dtype)
    main = jnp.concatenate([wq, zq], axis=-1)
    swp = jnp.concatenate([jnp.zeros_like(wq[..., :QK_NOPE]), wq[..., QK_NOPE + perm], zq], axis=-1)
    wuq = jnp.concatenate([main.reshape(depth, Q_RANK, -1), swp.reshape(depth, Q_RANK, -1)], axis=-1)
    wkv = w_ukv.reshape(depth, KV_RANK, HEADS, QK_NOPE + V_DIM)
    wuk = jnp.concatenate([wkv[..., :QK_NOPE], jnp.zeros((depth, KV_RANK, HEADS, HEAD_PAD - QK_NOPE), w_ukv.dtype)],
                          axis=-1).reshape(depth, KV_RANK, -1)
    wuv = wkv[..., QK_NOPE:].reshape(depth, KV_RANK, -1)
    wkr = w_in[:, :, OFF_KR:OFF_KR + QK_ROPE]
    zl = jnp.zeros((depth, D, ROPE_LANE0), w_in.dtype)
    zr = jnp.zeros((depth, D, HEAD_PAD - ROPE_LANE0 - QK_ROPE), w_in.dtype)
    wb = jnp.concatenate([w_in[:, :, OFF_CQ:OFF_KR], zl, wkr, zr, zl, wkr[:, :, perm], zr], axis=-1)
    eye = jnp.eye(C_BLOCKS, dtype=w_r.dtype)

    def dense(wblk):
        full = jnp.einsum("ldhij,hk->ldhikj", wblk, eye)
        return full.reshape(depth, 2, C_WIDTH, C_WIDTH)

    wri = jnp.concatenate([dense(w_r), dense(w_i)], axis=-1).reshape(depth * 2, C_WIDTH, 2 * C_WIDTH)
    bri = jnp.concatenate([b_r, b_i], axis=-1).reshape(depth * 2, 1, 2 * C_WIDTH)
    gw = A_WIDTH // A_GROUPS
    bs_full = jnp.broadcast_to(jnp.swapaxes(b_s, 1, 2)[:, :, :, None],
                               (depth, CHUNK, A_GROUPS, gw)).reshape(depth, CHUNK, A_WIDTH)
    return {
        "wa": w_in[:, :, OFF_AU:OFF_CQ].astype(BF16),
        "wb": wb.astype(BF16),
        "wc": w_in[:, :, OFF_CX:OFF_GATE].astype(BF16),
        "wg": w_in[:, :, OFF_GATE:].astype(BF16),
        "wuq": wuq.astype(BF16), "wuk": wuk.astype(BF16), "wuv": wuv.astype(BF16),
        "w_s": w_s.astype(BF16), "bs_full": bs_full,
        "wri": wri.astype(BF16), "bri": bri,
        "lam": lru_lambda.reshape(depth * 2, 1, C_WIDTH),
    }


def kernel(x, c, ctx, c_ctx, w_ada, b_ada, w_in, a_ln_g, a_ln_b, w_s, b_s, q_norm_g, w_uq,
           kv_norm_g, w_ukv, conv_w, conv_b, w_r, b_r, w_i, b_i, lru_lambda, w_br, w_o,
           ln1_g, ln1_b, w_router, router_bias, w_gate, w_up, w_down, ln2_g, ln2_b):
    depth = w_in.shape[0]
    bsz, seq, _ = x.shape
    ctx_len = ctx.shape[1]
    assert ctx_len == TILE and seq % TILE == 0 and bsz == SUBLANES
    t = ctx_len + seq
    nct = t // TILE
    n = bsz * t
    assert n % MM_TILE == 0 and (2 * n) % MOE_TILE == 0
    alpha = (2.0 * depth) ** 0.25

    row3 = lambda a: a.reshape(depth, 1, a.shape[-1])
    w = _pack_weights(w_in, w_uq, w_ukv, w_s, b_s, w_r, b_r, w_i, b_i, lru_lambda)
    w.update({
        "a_ln_g": row3(a_ln_g), "a_ln_b": row3(a_ln_b),
        "q_norm_g": row3(q_norm_g), "kv_norm_g": row3(kv_norm_g),
        "conv_w": conv_w, "conv_b": row3(conv_b),
        "w_br": w_br.astype(BF16), "w_o": w_o.astype(BF16),
        "ln1_g": row3(ln1_g), "ln1_b": row3(ln1_b), "ln2_g": row3(ln2_g), "ln2_b": row3(ln2_b),
        "w_gate": w_gate.astype(BF16), "w_up": w_up.astype(BF16), "w_down": w_down.astype(BF16),
    })
    cos_t, sin_t = _rope_tables(seq, ctx_len)
    wrt = w_router.T
    rbias = router_bias.reshape(N_EXPERTS, 1)

    cond_rows = jnp.zeros((2 * SUBLANES, D), F32).at[:bsz].set(c).at[bsz].set(c_ctx)
    mod = _mod_call(cond_rows, w_ada, b_ada)
    tile_row = np.array([bsz if j == 0 else b for b in range(bsz) for j in range(nct)], np.int32)
    mod_tiles = mod[:, tile_row, :].reshape(depth, n // TILE, 1, 6 * D)

    xs = jnp.concatenate([ctx, x], axis=1).reshape(n, D)
    for l in range(depth):
        mt = mod_tiles[l]
        h, ya, q, k, v, xc, cg = _proj_call(l, xs, mt, cos_t, sin_t, w, nct)
        yb = _attn_call(q, k, v, bsz, nct)
        rec_tm = _scan_call(l, jnp.swapaxes(xc.reshape(bsz, t, C_WIDTH), 0, 1), w, nct)
        rec = jnp.swapaxes(rec_tm, 1, 2).reshape(2, n, C_WIDTH)
        x1, f, re, rw, rr = _merge_call(l, xs, h, ya, yb, cg, rec, mt, w, wrt, rbias, alpha)
        pos, src_tok, tile_expert, nvalid = _dispatch_plan(re, rr, n // TILE)
        ys = _expert_call(l, f, src_tok, tile_expert, nvalid, w)
        xs = _combine_call(l, pos, ys, rw.T, x1, mt, w, alpha)
    return xs.reshape(bsz, t, D)[:, ctx_len:, :]
```
